```python
import jax, jax.numpy as jnp
from jax import lax
import numpy as np

D_MODEL = 1024
BATCH = 8
SEQ = 4096
DEPTH = 4

GRID_W = 64
CTX_LEN = 256
N_EVEN = (DEPTH + 1) // 2
N_ODD = DEPTH // 2
HEAD_DIM_A = 64
E_A = D_MODEL
HEADS_A = E_A // HEAD_DIM_A
E_B = D_MODEL
LORA_W = 64
LORA_A = 64
SHORT_CONV = 3
E_C = 2 * D_MODEL
CONF_CONV = 31
PROJ_EVEN = 4 * E_A + 4 * E_B
EVEN_SPLITS = [E_A, 2 * E_A, 3 * E_A, 4 * E_A, 4 * E_A + E_B, 4 * E_A + 2 * E_B, 4 * E_A + 3 * E_B]
RMS_EPS = 1e-6
LN_EPS = 1e-5
GN_EPS = 64e-5

kernel_name = "hybrid_rwkv7_shortconv_conformer_dit"


def rms_norm(x, g):
    xf = x.astype(jnp.float32)
    y = xf * lax.rsqrt(jnp.mean(jnp.square(xf), axis=-1, keepdims=True) + RMS_EPS)
    return (y * g.astype(jnp.float32)).astype(x.dtype)


def layer_norm(x, g, b):
    xf = x.astype(jnp.float32)
    mu = jnp.mean(xf, axis=-1, keepdims=True)
    var = jnp.mean(jnp.square(xf - mu), axis=-1, keepdims=True)
    return ((xf - mu) * lax.rsqrt(var + LN_EPS) * g + b).astype(x.dtype)


def head_group_norm(o, g, b):
    mu = jnp.mean(o, axis=-1, keepdims=True)
    var = jnp.mean(jnp.square(o - mu), axis=-1, keepdims=True)
    y = (o - mu) * lax.rsqrt(var + GN_EPS)
    return y.reshape(o.shape[:-2] + (-1,)) * g + b


def centred_shift(x):
    p = jnp.pad(x, ((0, 0), (1, 1), (0, 0)))
    return 0.5 * (p[:, :-2] + p[:, 2:])


def to_order(h, rows, col):
    if not col:
        return h
    b, t, ch = h.shape
    return h.reshape(b, rows, GRID_W, ch).swapaxes(1, 2).reshape(b, t, ch)


def from_order(h, rows, col):
    if not col:
        return h
    b, t, ch = h.shape
    return h.reshape(b, GRID_W, rows, ch).swapaxes(1, 2).reshape(b, t, ch)


def dwconv(x, w, seg_len):
    b, t, ch = x.shape
    k = w.shape[0]
    xs = x.reshape(b * (t // seg_len), seg_len, ch)
    y = lax.conv_general_dilated(xs, w[:, None, :].astype(x.dtype), window_strides=(1,),
                                 padding=[(k // 2, k // 2)],
                                 dimension_numbers=("NWC", "WIO", "NWC"),
                                 feature_group_count=ch)
    return y.reshape(b, t, ch)


def wkv_scan(S0, r, decay, k, v, kk, a, reverse):
    def step(S, inp):
        r_t, w_t, k_t, v_t, kk_t, a_t = inp
        sa = jnp.einsum("bhvk,bhk->bhv", S, -kk_t)
        S = (S * w_t[:, :, None, :]
             + sa[..., None] * (kk_t * a_t)[:, :, None, :]
             + v_t[..., None] * k_t[:, :, None, :])
        return S, jnp.einsum("bhvk,bhk->bhv", S, r_t)
    S, out = lax.scan(step, S0, (r, decay, k, v, kk, a), reverse=reverse)
    return out, S


def rwkv_prep(h, r, k, v, mu_rkv, mu_wa, w0, w1, w2, a0, a1, a2, k_k, k_a):
    f32 = jnp.float32
    b, t, _ = h.shape
    dh = centred_shift(h) - h
    xw = h + dh * mu_wa[0]
    xa = h + dh * mu_wa[1]
    r = r + (centred_shift(r) - r) * mu_rkv[0]
    k = k + (centred_shift(k) - k) * mu_rkv[1]
    v = v + (centred_shift(v) - v) * mu_rkv[2]
    lw = jnp.einsum("zbtr,zre->zbte", jnp.tanh(jnp.einsum("btd,zdr->zbtr", xw, w1)), w2)
    w_log = -jax.nn.softplus(-(w0[:, None, None, :] + lw).astype(f32)) - 0.5
    decay = jnp.exp(-jnp.exp(w_log))
    la = jnp.einsum("zbtr,zre->zbte", jnp.einsum("btd,zdr->zbtr", xa, a1), a2)
    a = jax.nn.sigmoid((a0[:, None, None, :] + la).astype(f32))
    k = k.astype(f32)
    kk = (k * k_k).reshape(b, t, HEADS_A, HEAD_DIM_A)
    kk = kk * lax.rsqrt(jnp.maximum(jnp.sum(kk * kk, axis=-1, keepdims=True), 1e-24))
    kd = k[None] * (1.0 + (a - 1.0) * k_a)
    heads = lambda z: z.reshape(z.shape[:-1] + (HEADS_A, HEAD_DIM_A))
    return heads(r.astype(f32)), heads(v.astype(f32)), kk, heads(decay), heads(kd), heads(a)


def rwkv_seq(prep, S0_f, S0_b, r_k, lnx_w, lnx_b, with_output):
    r, v, kk, decay, kd, a = prep
    tm = lambda z: jnp.swapaxes(z, 0, 1)
    rt, vt, kkt = tm(r), tm(v), tm(kk)
    out_f, S_f = wkv_scan(S0_f, rt, tm(decay[0]), tm(kd[0]), vt, kkt, tm(a[0]), False)
    out_b, S_b = wkv_scan(S0_b, rt, tm(decay[1]), tm(kd[1]), vt, kkt, tm(a[1]), True)
    if not with_output:
        return None, S_f, S_b
    o = tm(out_f + out_b)
    bonus = jnp.sum(r[None] * kd * r_k[:, None, None], axis=-1, keepdims=True) * v[None]
    y = head_group_norm(o, lnx_w, lnx_b) + bonus.sum(0).reshape(o.shape[:-2] + (-1,))
    return y, S_f, S_b


def even_mixer(h, hc, seg_len, ctx_out, w_in, w_out, mu_rkv, mu_wa, w0, w1, w2, a0, a1, a2,
               k_k, k_a, r_k, lnx_w, lnx_b, sc_w):
    rw = (mu_rkv, mu_wa, w0, w1, w2, a0, a1, a2, k_k, k_a)
    b = h.shape[0]
    zero = jnp.zeros((b, HEADS_A, HEAD_DIM_A, HEAD_DIM_A), jnp.float32)
    rc, kc, vc, zac, bc, cc, xc, zbc = jnp.split(hc @ w_in, EVEN_SPLITS, axis=-1)
    yac, S_f, S_b = rwkv_seq(rwkv_prep(hc, rc, kc, vc, *rw), zero, zero, r_k, lnx_w, lnx_b, ctx_out)
    r, k, v, za, bg, cg, xb, zb = jnp.split(h @ w_in, EVEN_SPLITS, axis=-1)
    ya, _, _ = rwkv_seq(rwkv_prep(h, r, k, v, *rw), S_f, S_b, r_k, lnx_w, lnx_b, True)
    yb = bg * dwconv(cg * xb, sc_w, seg_len)
    y = jnp.concatenate([ya.astype(h.dtype) * jax.nn.silu(za), yb * jax.nn.silu(zb)], axis=-1) @ w_out
    yc = None
    if ctx_out:
        ybc = bc * dwconv(cc * xc, sc_w, hc.shape[1])
        yc = jnp.concatenate([yac.astype(hc.dtype) * jax.nn.silu(zac), ybc * jax.nn.silu(zbc)], axis=-1) @ w_out
    return y, yc


def odd_mixer(h, seg_len, w_in, dw_w, dw_b, ln_w, ln_b, w_out):
    u, gl, z = jnp.split(h @ w_in, 3, axis=-1)
    a = u * jax.nn.sigmoid(gl)
    a = dwconv(a, dw_w, seg_len) + dw_b
    a = layer_norm(a, ln_w, ln_b)
    return (jax.nn.silu(a) * jax.nn.silu(z)) @ w_out


def setup_inputs(seed: int = 0) -> dict:
    key = jax.random.key(seed)
    ks = iter(jax.random.split(key, 40))
    nrm = lambda shape, s: jax.random.normal(next(ks), shape, jnp.float32) * s
    uni = lambda shape, lo, hi: jax.random.uniform(next(ks), shape, jnp.float32, lo, hi)
    D = D_MODEL
    return {
        "x": nrm((BATCH, SEQ, D), 1.0),
        "c": nrm((BATCH, D), 1.0),
        "ctx": nrm((BATCH, CTX_LEN, D), 1.0),
        "c_ctx": nrm((D,), 1.0),
        "mod_w": nrm((DEPTH, D, 3 * D), 0.5 * D ** -0.5),
        "mod_b": nrm((DEPTH, 3 * D), 0.01),
        "g_pre": 1.0 + nrm((DEPTH, D), 0.02),
        "g_post": 1.0 + nrm((DEPTH, D), 0.02),
        "ev_w_in": nrm((N_EVEN, D, PROJ_EVEN), D ** -0.5),
        "ev_w_out": nrm((N_EVEN, E_A + E_B, D), (E_A + E_B) ** -0.5),
        "ev_mu_rkv": uni((N_EVEN, 3, E_A), 0.0, 1.0),
        "ev_mu_wa": uni((N_EVEN, 2, D), 0.0, 1.0),
        "ev_w0": uni((N_EVEN, 2, E_A), -6.0, -0.5),
        "ev_w1": nrm((N_EVEN, 2, D, LORA_W), D ** -0.5),
        "ev_w2": nrm((N_EVEN, 2, LORA_W, E_A), 0.5 * LORA_W ** -0.5),
        "ev_a0": nrm((N_EVEN, 2, E_A), 0.1),
        "ev_a1": nrm((N_EVEN, 2, D, LORA_A), D ** -0.5),
        "ev_a2": nrm((N_EVEN, 2, LORA_A, E_A), 0.5 * LORA_A ** -0.5),
        "ev_k_k": 0.85 + nrm((N_EVEN, E_A), 0.05),
        "ev_k_a": 1.0 + nrm((N_EVEN, E_A), 0.05),
        "ev_r_k": nrm((N_EVEN, 2, HEADS_A, HEAD_DIM_A), 0.1),
        "ev_lnx_w": 1.0 + nrm((N_EVEN, E_A), 0.02),
        "ev_lnx_b": nrm((N_EVEN, E_A), 0.01),
        "ev_sc_w": nrm((N_EVEN, SHORT_CONV, E_B), SHORT_CONV ** -0.5),
        "od_w_in": nrm((N_ODD, D, 3 * E_C), D ** -0.5),
        "od_dw_w": nrm((N_ODD, CONF_CONV, E_C), CONF_CONV ** -0.5),
        "od_dw_b": nrm((N_ODD, E_C), 0.01),
        "od_ln_w": 1.0 + nrm((N_ODD, E_C), 0.02),
        "od_ln_b": nrm((N_ODD, E_C), 0.01),
        "od_w_out": nrm((N_ODD, E_C, D), E_C ** -0.5),
    }


def reference(x, c, ctx, c_ctx, mod_w, mod_b, g_pre, g_post, ev_w_in, ev_w_out, ev_mu_rkv, ev_mu_wa,
              ev_w0, ev_w1, ev_w2, ev_a0, ev_a1, ev_a2, ev_k_k, ev_k_a, ev_r_k, ev_lnx_w, ev_lnx_b,
              ev_sc_w, od_w_in, od_dw_w, od_dw_b, od_ln_w, od_ln_b, od_w_out):
    b, t, _ = x.shape
    rows = t // GRID_W
    n_ctx = ctx.shape[1]
    s_lat = jax.nn.silu(c)
    s_ctx = jax.nn.silu(c_ctx)
    for i in range(DEPTH):
        j = i // 2
        last = i == DEPTH - 1
        even = i % 2 == 0
        col = j % 2 == 1
        seg = rows if col else GRID_W
        shift, scale, gate = jnp.split(s_lat @ mod_w[i] + mod_b[i], 3, axis=-1)
        h = rms_norm(x, g_pre[i]) * (1.0 + scale[:, None]) + shift[:, None]
        h = to_order(h, rows, col)
        need_ctx = even or not last
        if need_ctx:
            shift_c, scale_c, gate_c = jnp.split(s_ctx @ mod_w[i] + mod_b[i], 3, axis=-1)
            hc = rms_norm(ctx, g_pre[i]) * (1.0 + scale_c) + shift_c
        if even:
            y, yc = even_mixer(h, hc, seg, not last, ev_w_in[j], ev_w_out[j], ev_mu_rkv[j], ev_mu_wa[j],
                               ev_w0[j], ev_w1[j], ev_w2[j], ev_a0[j], ev_a1[j], ev_a2[j], ev_k_k[j],
                               ev_k_a[j], ev_r_k[j], ev_lnx_w[j], ev_lnx_b[j], ev_sc_w[j])
        else:
            y = odd_mixer(h, seg, od_w_in[j], od_dw_w[j], od_dw_b[j], od_ln_w[j], od_ln_b[j], od_w_out[j])
            yc = None
            if not last:
                yc = odd_mixer(hc, n_ctx, od_w_in[j], od_dw_w[j], od_dw_b[j], od_ln_w[j], od_ln_b[j], od_w_out[j])
        x = x + gate[:, None] * rms_norm(from_order(y, rows, col), g_post[i])
        if not last:
            ctx = ctx + gate_c * rms_norm(yc, g_post[i])
    return x
```

```python
import functools

import jax
import jax.numpy as jnp
from jax import lax
from jax.experimental import pallas as pl
from jax.experimental.pallas import tpu as pltpu

F32 = jnp.float32
BF16 = jnp.bfloat16

GRID_W = 64
HEAD = 64
PAIR = 2 * HEAD
CHUNK = 64
LORA = 64
RMS_EPS = 1e-6
LN_EPS = 1e-5
GN_EPS = 64e-5
HALO = 8
TOKEN_TILE = 256
SCAN_TILE = 256
CONV_PAD = 16
CONV_COLS = 512
VMEM_LIMIT = 56 * 1024 * 1024


def _dot(a, b):
    return jnp.dot(a.astype(BF16), b.astype(BF16), preferred_element_type=F32)


def _dot_nt(a, b):
    return lax.dot_general(a.astype(BF16), b.astype(BF16), (((1,), (1,)), ((), ())),
                           preferred_element_type=F32)


def _dot_tn(a, b):
    return lax.dot_general(a.astype(BF16), b.astype(BF16), (((0,), (0,)), ((), ())),
                           preferred_element_type=F32)


def _split3(x):
    hi = x.astype(BF16)
    r1 = x - hi.astype(F32)
    mid = r1.astype(BF16)
    lo = (r1 - mid.astype(F32)).astype(BF16)
    return hi, mid, lo


def _dot_x_exact(x, e):
    hi, mid, lo = _split3(x)
    f = lambda p: jnp.dot(p, e, preferred_element_type=F32)
    return f(hi) + f(mid) + f(lo)


def _dot_exact_x(e, x):
    hi, mid, lo = _split3(x)
    f = lambda p: jnp.dot(e, p, preferred_element_type=F32)
    return f(hi) + f(mid) + f(lo)


def _sigmoid(x):
    return 1.0 / (1.0 + jnp.exp(-x))


def _silu(x):
    return x * _sigmoid(x)


def _rms(x, g):
    return x * lax.rsqrt(jnp.mean(x * x, axis=-1, keepdims=True) + RMS_EPS) * g


def _modulated(x, mod_ref, g_ref):
    return _rms(x, g_ref[...]) * (1.0 + mod_ref[0, 1:2, :]) + mod_ref[0, 0:1, :]


def _const_spec(shape):
    return pl.BlockSpec(shape, lambda *_: (0,) * len(shape), pipeline_mode=pl.Buffered(1))


def _tok_spec(tt, d):
    return pl.BlockSpec((1, tt, d), lambda b, t: (b, t, 0))


def _mod_spec(d):
    return pl.BlockSpec((1, 3, d), lambda b, t: (b, 0, 0))


def _mod_kernel(c_ref, w_ref, b_ref, o_ref):
    o_ref[0] = _dot(_silu(c_ref[...]), w_ref[0]) + b_ref[0]


def _modulation(c_all, mod_w, mod_b):
    depth, d, d3 = mod_w.shape
    rows = c_all.shape[0]
    return pl.pallas_call(
        _mod_kernel,
        grid=(depth, d3 // d),
        in_specs=[pl.BlockSpec((rows, d), lambda i, n: (0, 0)),
                  pl.BlockSpec((1, d, d), lambda i, n: (i, 0, n)),
                  pl.BlockSpec((1, 1, d), lambda i, n: (i, 0, n))],
        out_specs=pl.BlockSpec((1, rows, d), lambda i, n: (i, 0, n)),
        out_shape=jax.ShapeDtypeStruct((depth, rows, d3), F32),
        name="modulation",
    )(c_all, mod_w, mod_b.reshape(depth, 1, d3))


def _rwkv_proj_kernel(x_ref, xp_ref, xn_ref, mod_ref, g_ref, w_ref, w1_ref, w2_ref, a1_ref, a2_ref,
                      mu_rkv_ref, mu_wa_ref, w0_ref, a0_ref, kk_ref, ka_ref, rk_ref, ones_ref,
                      r_out, k_out, v_out, kkn_out, sza_out, bonus_out, lw_out, a_out,
                      hext_ref, pext_ref):
    t = pl.program_id(1)
    nt = pl.num_programs(1)
    tt, d = x_ref.shape[1], x_ref.shape[2]
    h = _modulated(x_ref[0], mod_ref, g_ref)
    hext_ref[0:HALO] = jnp.where(t > 0, _modulated(xp_ref[0], mod_ref, g_ref), 0.0)
    hext_ref[HALO:HALO + tt] = h
    hext_ref[HALO + tt:] = jnp.where(t < nt - 1, _modulated(xn_ref[0], mod_ref, g_ref), 0.0)

    def shifted(ref):
        return 0.5 * (ref[pl.ds(HALO - 1, tt), :] + ref[pl.ds(HALO + 1, tt), :])

    dh = shifted(hext_ref) - h
    lane = lax.broadcasted_iota(jnp.int32, (tt, 2 * LORA), 1)

    def lora(xin, down_ref, up_ref, act):
        mid = act(_dot(xin, down_ref[...]))
        up = up_ref[...]
        return (_dot(jnp.where(lane < LORA, mid, 0.0), up), _dot(jnp.where(lane >= LORA, mid, 0.0), up))

    lw = lora(h + dh * mu_wa_ref[0:1, :], w1_ref, w2_ref, jnp.tanh)
    la = lora(h + dh * mu_wa_ref[1:2, :], a1_ref, a2_ref, lambda z: z)
    a_dir = []
    for z in range(2):
        lw_out[z, 0] = -jnp.exp(-0.5) * _sigmoid(w0_ref[z:z + 1, :] + lw[z])
        a_z = _sigmoid(a0_ref[z:z + 1, :] + la[z])
        a_out[z, 0] = a_z
        a_dir.append(a_z)

    hext_b = hext_ref[...].astype(BF16)
    hb = h.astype(BF16)
    mixed = []
    for g in range(3):
        pext_ref[...] = jnp.dot(hext_b, w_ref[:, g * d:(g + 1) * d], preferred_element_type=F32)
        centre = pext_ref[pl.ds(HALO, tt), :]
        mixed.append(centre + (shifted(pext_ref) - centre) * mu_rkv_ref[g:g + 1, :])
    r, k, v = mixed
    r_out[0] = r
    k_out[0] = k
    v_out[0] = v
    sza_out[0] = _silu(jnp.dot(hb, w_ref[:, 3 * d:4 * d], preferred_element_type=F32))

    ones = ones_ref[...]
    kq = k * kk_ref[...]
    kkn_out[0] = kq * lax.rsqrt(jnp.maximum(_dot_x_exact(kq * kq, ones), 1e-24))
    ka = ka_ref[...]
    bsum = r * k * ((1.0 + (a_dir[0] - 1.0) * ka) * rk_ref[0:1, :] + (1.0 + (a_dir[1] - 1.0) * ka) * rk_ref[1:2, :])
    bonus_out[0] = _dot_x_exact(bsum, ones) * v


def _rwkv_proj(x, mod, g_pre, w_rkvz, w1c, w2c, a1c, a2c, mu_rkv, mu_wa, w0, a0, k_k, k_a, r_k, ones, tt):
    b, t, d = x.shape
    nh = t // HALO
    step = tt // HALO
    vec = lambda n: _const_spec((n, d))
    tok = _tok_spec(tt, d)
    tok2 = pl.BlockSpec((2, 1, tt, d), lambda bi, ti: (0, bi, ti, 0))
    sd = jax.ShapeDtypeStruct((b, t, d), F32)
    sd2 = jax.ShapeDtypeStruct((2, b, t, d), F32)
    return pl.pallas_call(
        _rwkv_proj_kernel,
        grid=(b, t // tt),
        in_specs=[tok,
                  pl.BlockSpec((1, HALO, d), lambda bi, ti: (bi, jnp.maximum(ti * step - 1, 0), 0)),
                  pl.BlockSpec((1, HALO, d), lambda bi, ti: (bi, jnp.minimum((ti + 1) * step, nh - 1), 0)),
                  _mod_spec(d), vec(1),
                  _const_spec((d, 4 * d)), _const_spec((d, 2 * LORA)), _const_spec((2 * LORA, d)),
                  _const_spec((d, 2 * LORA)), _const_spec((2 * LORA, d)),
                  vec(3), vec(2), vec(2), vec(2), vec(1), vec(1), vec(2), _const_spec((d, d))],
        out_specs=[tok, tok, tok, tok, tok, tok, tok2, tok2],
        out_shape=[sd, sd, sd, sd, sd, sd, sd2, sd2],
        scratch_shapes=[pltpu.VMEM((tt + 2 * HALO, d), F32), pltpu.VMEM((tt + 2 * HALO, d), F32)],
        compiler_params=pltpu.CompilerParams(dimension_semantics=("parallel", "parallel"),
                                             vmem_limit_bytes=VMEM_LIMIT),
        name="rwkv_proj",
    )(x, x, x, mod, g_pre, w_rkvz, w1c, w2c, a1c, a2c, mu_rkv, mu_wa, w0, a0, k_k, k_a, r_k, ones)


def _sconv_kernel(x_ref, mod_ref, g_ref, w_ref, scw_ref, o_ref, uext_ref, *, seg):
    tt, d = x_ref.shape[1], x_ref.shape[2]
    hb = _modulated(x_ref[0], mod_ref, g_ref).astype(BF16)
    proj = lambda i: jnp.dot(hb, w_ref[:, i * d:(i + 1) * d], preferred_element_type=F32)
    u = proj(1) * proj(2)
    zero = jnp.zeros((HALO, d), F32)
    uext_ref[0:HALO] = zero
    uext_ref[HALO:HALO + tt] = u
    uext_ref[HALO + tt:] = zero
    pos = lax.broadcasted_iota(jnp.int32, (tt, d), 0) & (seg - 1)
    prev = jnp.where(pos > 0, uext_ref[pl.ds(HALO - 1, tt), :], 0.0)
    nxt = jnp.where(pos < seg - 1, uext_ref[pl.ds(HALO + 1, tt), :], 0.0)
    conv = scw_ref[0:1, :] * prev + scw_ref[1:2, :] * u + scw_ref[2:3, :] * nxt
    o_ref[0] = proj(0) * conv * _silu(proj(3))


def _sconv(x, mod, g_pre, w_sc, sc_w, seg, tt):
    b, t, d = x.shape
    return pl.pallas_call(
        functools.partial(_sconv_kernel, seg=seg),
        grid=(b, t // tt),
        in_specs=[_tok_spec(tt, d), _mod_spec(d), _const_spec((1, d)), _const_spec((d, 4 * d)),
                  _const_spec((3, d))],
        out_specs=_tok_spec(tt, d),
        out_shape=jax.ShapeDtypeStruct((b, t, d), F32),
        scratch_shapes=[pltpu.VMEM((tt + 2 * HALO, d), F32)],
        compiler_params=pltpu.CompilerParams(dimension_semantics=("parallel", "parallel"),
                                             vmem_limit_bytes=VMEM_LIMIT),
        name="sconv",
    )(x, mod, g_pre, w_sc, sc_w)


def _unit_triangular_inverse(a, blk16):
    n = a.shape[0]
    eye = (lax.broadcasted_iota(jnp.int32, (n, n), 0) == lax.broadcasted_iota(jnp.int32, (n, n), 1)).astype(F32)
    dg = jnp.where(blk16, a, 0.0)
    off = a - dg
    d2 = _dot(dg, dg)
    d4 = _dot(d2, d2)
    d8 = _dot(d4, d4)
    tm = eye + dg
    tm = tm + _dot(tm, d2)
    tm = tm + _dot(tm, d4)
    tm = tm + _dot(tm, d8)
    nb = _dot(tm, off)
    nb2 = _dot(nb, nb)
    return _dot(eye + nb + nb2 + _dot(nb, nb2), tm)


def _wkv_kernel(s0_ref, r_ref, k_ref, v_ref, kk_ref, lw_ref, a_ref, ka_ref, o_ref, s_out, s_ref, *, reverse):
    t = pl.program_id(2)
    nt = pl.num_programs(2)
    tt = r_ref.shape[1]
    nc = tt // CHUNK

    @pl.when(t == 0)
    def _():
        s_ref[...] = s0_ref[0, 0]

    ri = lax.broadcasted_iota(jnp.int32, (PAIR, PAIR), 0)
    ci = lax.broadcasted_iota(jnp.int32, (PAIR, PAIR), 1)
    same = (ri >> 6) == (ci >> 6)
    blk16 = (ri >> 4) == (ci >> 4)
    rc, cc = lax.broadcasted_iota(jnp.int32, (CHUNK, CHUNK), 0), lax.broadcasted_iota(jnp.int32, (CHUNK, CHUNK), 1)
    if reverse:
        strict, incl, tri = same & (ri < ci), same & (ri <= ci), (rc <= cc).astype(BF16)
    else:
        strict, incl, tri = same & (ri > ci), same & (ri >= ci), (rc >= cc).astype(BF16)
    first_head = lax.broadcasted_iota(jnp.int32, (CHUNK, PAIR), 1) < HEAD
    stack = lambda z: jnp.concatenate([jnp.where(first_head, z, 0.0), jnp.where(first_head, 0.0, z)], axis=0)
    ka = ka_ref[...]

    s = s_ref[...]
    for j in (range(nc - 1, -1, -1) if reverse else range(nc)):
        rows = pl.ds(j * CHUNK, CHUNK)
        lw, a, kkc = lw_ref[0, 0, rows, :], a_ref[0, 0, rows, :], kk_ref[0, rows, :]
        cum = _dot_exact_x(tri, lw)
        g_inv = jnp.exp(-cum)
        g_end = jnp.exp(cum[0:1] if reverse else cum[CHUNK - 1:CHUNK])
        a2 = stack(-kkc * jnp.exp(cum - lw))
        r2 = stack(r_ref[0, rows, :] * jnp.exp(cum))
        b2 = stack(kkc * a * g_inv)
        k2 = stack(k_ref[0, rows, :] * (1.0 + (a - 1.0) * ka) * g_inv)
        v2 = stack(v_ref[0, rows, :])
        ar = jnp.concatenate([a2, r2], axis=0)
        sc = _dot_nt(ar, jnp.concatenate([b2, k2], axis=0))
        a_ab = jnp.where(strict, sc[0:PAIR, 0:PAIR], 0.0)
        a_ak = jnp.where(strict, sc[0:PAIR, PAIR:], 0.0)
        a_rbk = jnp.concatenate([jnp.where(incl, sc[PAIR:, 0:PAIR], 0.0), jnp.where(incl, sc[PAIR:, PAIR:], 0.0)], axis=1)
        tm = _unit_triangular_inverse(a_ab, blk16)
        sh = _dot_nt(ar, s)
        u2 = _dot(tm, sh[0:PAIR] + _dot(a_ak, v2))
        uv = jnp.concatenate([u2, v2], axis=0)
        o2 = sh[PAIR:] + _dot(a_rbk, uv)
        o_ref[0, rows, :] = o2[0:CHUNK] + o2[CHUNK:]
        s = s * g_end + _dot_tn(uv, jnp.concatenate([b2 * g_end, k2 * g_end], axis=0))
    s_ref[...] = s

    @pl.when(t == nt - 1)
    def _():
        s_out[0, 0] = s


def _wkv(s0, r, k, v, kk, lw, a, k_a, reverse, tt):
    b, t, d = r.shape
    nt = t // tt
    z = 1 if reverse else 0
    tpos = (lambda ti: nt - 1 - ti) if reverse else (lambda ti: ti)
    tok = pl.BlockSpec((1, tt, PAIR), lambda bi, p, ti: (bi, tpos(ti), p))
    tokz = pl.BlockSpec((1, 1, tt, PAIR), lambda bi, p, ti: (z, bi, tpos(ti), p))
    st = pl.BlockSpec((1, 1, PAIR, PAIR), lambda bi, p, ti: (bi, p, 0, 0))
    return pl.pallas_call(
        functools.partial(_wkv_kernel, reverse=reverse),
        grid=(b, d // PAIR, nt),
        in_specs=[st, tok, tok, tok, tok, tokz, tokz, pl.BlockSpec((1, PAIR), lambda bi, p, ti: (0, p))],
        out_specs=[tok, st],
        out_shape=[jax.ShapeDtypeStruct((b, t, d), F32), jax.ShapeDtypeStruct((b, d // PAIR, PAIR, PAIR), F32)],
        scratch_shapes=[pltpu.VMEM((PAIR, PAIR), F32)],
        compiler_params=pltpu.CompilerParams(dimension_semantics=("parallel", "parallel", "arbitrary")),
        name="wkv_bwd" if reverse else "wkv_fwd",
    )(s0, r, k, v, kk, lw, a, k_a)


def _even_out_kernel(x_ref, of_ref, ob_ref, bonus_ref, sza_ref, yb_ref, mod_ref, lnw_ref, lnb_ref, gp_ref,
                     w_ref, ones_ref, o_ref):
    d = x_ref.shape[2]
    ones = ones_ref[...]
    o = of_ref[0] + ob_ref[0]
    dev = o - _dot_x_exact(o, ones) * (1.0 / HEAD)
    var = _dot_x_exact(dev * dev, ones) * (1.0 / HEAD)
    ya = (dev * lax.rsqrt(var + GN_EPS) * lnw_ref[...] + lnb_ref[...] + bonus_ref[0]) * sza_ref[0]
    y = _dot(ya, w_ref[0:d, :]) + _dot(yb_ref[0], w_ref[d:2 * d, :])
    o_ref[0] = x_ref[0] + mod_ref[0, 2:3, :] * _rms(y, gp_ref[...])


def _even_out(x, o_f, o_b, bonus, sza, ybg, mod, lnx_w, lnx_b, g_post, w_out, ones, tt):
    b, t, d = x.shape
    tok = _tok_spec(tt, d)
    vec = _const_spec((1, d))
    return pl.pallas_call(
        _even_out_kernel,
        grid=(b, t // tt),
        in_specs=[tok, tok, tok, tok, tok, tok, _mod_spec(d), vec, vec, vec,
                  _const_spec((2 * d, d)), _const_spec((d, d))],
        out_specs=tok,
        out_shape=jax.ShapeDtypeStruct((b, t, d), F32),
        compiler_params=pltpu.CompilerParams(dimension_semantics=("parallel", "parallel"),
                                             vmem_limit_bytes=VMEM_LIMIT),
        name="even_out",
    )(x, o_f, o_b, bonus, sza, ybg, mod, lnx_w, lnx_b, g_post, w_out, ones)


def _conformer_kernel(x_ref, mod_ref, g_ref, w_in_ref, dww_ref, dwb_ref, lnw_ref, lnb_ref, w_out_ref, gp_ref,
                      o_ref, pad_ref, conv_ref, *, seg):
    tt, d = x_ref.shape[1], x_ref.shape[2]
    e = dww_ref.shape[1]
    taps = dww_ref.shape[0]
    span = seg + 2 * CONV_PAD
    x = x_ref[0]
    hb = _modulated(x, mod_ref, g_ref).astype(BF16)
    proj = lambda i: jnp.dot(hb, w_in_ref[:, i * e:(i + 1) * e], preferred_element_type=F32)
    glu = proj(0) * _sigmoid(proj(1))
    zero = jnp.zeros((CONV_PAD, e), F32)
    for s in range(tt // seg):
        pad_ref[s * span:s * span + CONV_PAD] = zero
        pad_ref[s * span + CONV_PAD:s * span + CONV_PAD + seg] = glu[s * seg:(s + 1) * seg]
        pad_ref[s * span + CONV_PAD + seg:(s + 1) * span] = zero

    rows = min(seg, CHUNK)
    for s in range(tt // seg):
        for r0 in range(0, seg, rows):
            for c0 in range(0, e, CONV_COLS):
                cols = pl.ds(c0, CONV_COLS)
                acc = jnp.zeros((rows, CONV_COLS), F32)
                for j in range(taps):
                    off = s * span + CONV_PAD + r0 + j - taps // 2
                    acc = acc + dww_ref[j:j + 1, cols] * pad_ref[pl.ds(off, rows), cols]
                conv_ref[pl.ds(s * seg + r0, rows), cols] = acc

    conv = conv_ref[...] + dwb_ref[...]
    dev = conv - jnp.mean(conv, axis=-1, keepdims=True)
    var = jnp.mean(dev * dev, axis=-1, keepdims=True)
    act = _silu(dev * lax.rsqrt(var + LN_EPS) * lnw_ref[...] + lnb_ref[...]) * _silu(proj(2))
    o_ref[0] = x + mod_ref[0, 2:3, :] * _rms(_dot(act, w_out_ref[...]), gp_ref[...])


def _conformer(x, mod, g_pre, w_in, dw_w, dw_b, ln_w, ln_b, w_out, g_post, seg, tt):
    b, t, d = x.shape
    taps, e = dw_w.shape
    assert taps // 2 <= CONV_PAD and tt % seg == 0
    tok = _tok_spec(tt, d)
    return pl.pallas_call(
        functools.partial(_conformer_kernel, seg=seg),
        grid=(b, t // tt),
        in_specs=[tok, _mod_spec(d), _const_spec((1, d)), _const_spec((d, 3 * e)), _const_spec((taps, e)),
                  _const_spec((1, e)), _const_spec((1, e)), _const_spec((1, e)), _const_spec((e, d)),
                  _const_spec((1, d))],
        out_specs=tok,
        out_shape=jax.ShapeDtypeStruct((b, t, d), F32),
        scratch_shapes=[pltpu.VMEM(((tt // seg) * (seg + 2 * CONV_PAD), e), F32), pltpu.VMEM((tt, e), F32)],
        compiler_params=pltpu.CompilerParams(dimension_semantics=("parallel", "parallel"),
                                             vmem_limit_bytes=VMEM_LIMIT),
        name="conformer",
    )(x, mod, g_pre, w_in, dw_w, dw_b, ln_w, ln_b, w_out, g_post)


def _to_col_major(h, rows):
    b, t, ch = h.shape
    return h.reshape(b, rows, GRID_W, ch).swapaxes(1, 2).reshape(b, t, ch)


def _to_row_major(h, rows):
    b, t, ch = h.shape
    return h.reshape(b, GRID_W, rows, ch).swapaxes(1, 2).reshape(b, t, ch)


def kernel(x, c, ctx, c_ctx, mod_w, mod_b, g_pre, g_post, ev_w_in, ev_w_out, ev_mu_rkv, ev_mu_wa, ev_w0, ev_w1, ev_w2, ev_a0, ev_a1, ev_a2, ev_k_k, ev_k_a, ev_r_k, ev_lnx_w, ev_lnx_b, ev_sc_w, od_w_in, od_dw_w, od_dw_b, od_ln_w, od_ln_b, od_w_out):
    b, t, d = x.shape
    n_ctx = ctx.shape[1]
    depth = mod_w.shape[0]
    rows = t // GRID_W
    assert d % PAIR == 0 and t % TOKEN_TILE == 0 and n_ctx % CHUNK == 0
    tt_ctx = min(TOKEN_TILE, n_ctx)

    mod_rows = -(-(b + 1) // 8) * 8
    c_all = jnp.zeros((mod_rows, d), F32).at[:b].set(c).at[b].set(c_ctx)
    mods = _modulation(c_all, mod_w, mod_b).reshape(depth, mod_rows, 3, d)
    hid = jnp.arange(d) // HEAD
    ones = (hid[:, None] == hid[None, :]).astype(BF16)

    col_major = False
    for i in range(depth):
        j = i // 2
        last = i == depth - 1
        col = j % 2 == 1
        if col != col_major:
            x = _to_col_major(x, rows) if col else _to_row_major(x, rows)
            col_major = col
        seg = rows if col else GRID_W
        mod_x = mods[i, :b]
        mod_c = jnp.broadcast_to(mods[i, b], (b, 3, d))
        gpre, gpost = g_pre[i][None], g_post[i][None]
        if i % 2 == 0:
            w_in = ev_w_in[j].astype(BF16)
            w_rkvz, w_sc = w_in[:, :4 * d], w_in[:, 4 * d:]
            w_out = ev_w_out[j].astype(BF16)
            cat = lambda w: jnp.concatenate([w[0], w[1]], axis=1).astype(BF16)
            stk = lambda w: jnp.concatenate([w[0], w[1]], axis=0).astype(BF16)
            k_a = ev_k_a[j][None]
            proj_args = (gpre, w_rkvz, cat(ev_w1[j]), stk(ev_w2[j]), cat(ev_a1[j]), stk(ev_a2[j]), ev_mu_rkv[j],
                         ev_mu_wa[j], ev_w0[j], ev_a0[j], ev_k_k[j][None], k_a, ev_r_k[j].reshape(2, d), ones)
            pc = _rwkv_proj(ctx, mod_c, *proj_args, tt_ctx)
            px = _rwkv_proj(x, mod_x, *proj_args, TOKEN_TILE)
            zero = jnp.zeros((b, d // PAIR, PAIR, PAIR), F32)
            scan = lambda p, s0, rev, tt: _wkv(s0, p[0], p[1], p[2], p[3], p[6], p[7], k_a, rev, tt)
            oc_f, s_f = scan(pc, zero, False, min(SCAN_TILE, n_ctx))
            oc_b, s_b = scan(pc, zero, True, min(SCAN_TILE, n_ctx))
            ox_f, _ = scan(px, s_f, False, SCAN_TILE)
            ox_b, _ = scan(px, s_b, True, SCAN_TILE)
            yb_x = _sconv(x, mod_x, gpre, w_sc, ev_sc_w[j], seg, TOKEN_TILE)
            out_args = (ev_lnx_w[j][None], ev_lnx_b[j][None], gpost, w_out, ones)
            x = _even_out(x, ox_f, ox_b, px[5], px[4], yb_x, mod_x, *out_args, TOKEN_TILE)
            if not last:
                yb_c = _sconv(ctx, mod_c, gpre, w_sc, ev_sc_w[j], n_ctx, tt_ctx)
                ctx = _even_out(ctx, oc_f, oc_b, pc[5], pc[4], yb_c, mod_c, *out_args, tt_ctx)
        else:
            odd_args = (gpre, od_w_in[j].astype(BF16), od_dw_w[j], od_dw_b[j][None], od_ln_w[j][None],
                        od_ln_b[j][None], od_w_out[j].astype(BF16), gpost)
            if not last:
                ctx_new = _conformer(ctx, mod_c, *odd_args, n_ctx, tt_ctx)
            x = _conformer(x, mod_x, *odd_args, seg, TOKEN_TILE)
            if not last:
                ctx = ctx_new
    if col_major:
        x = _to_row_major(x, rows)
    return x
```

```python
import functools

import jax
import jax.numpy as jnp
from jax import lax
from jax.experimental import pallas as pl
from jax.experimental.pallas import tpu as pltpu

F32 = jnp.float32
BF16 = jnp.bfloat16

GRID_W = 64
HEAD = 64
PAIR = 2 * HEAD
CHUNK = 64
LORA = 64
RMS_EPS = 1e-6
LN_EPS = 1e-5
GN_EPS = 64e-5
HALO = 8
TOKEN_TILE = 256
SCAN_TILE = 256
SCAN_PAIRS = 2
CONV_PAD = 16
CONV_COLS = 512
VMEM_LIMIT = 56 * 1024 * 1024


def _dot(a, b):
    return jnp.dot(a.astype(BF16), b.astype(BF16), preferred_element_type=F32)


def _dot_nt(a, b):
    return lax.dot_general(a.astype(BF16), b.astype(BF16), (((1,), (1,)), ((), ())),
                           preferred_element_type=F32)


def _dot_tn(a, b):
    return lax.dot_general(a.astype(BF16), b.astype(BF16), (((0,), (0,)), ((), ())),
                           preferred_element_type=F32)


def _split3(x):
    hi = x.astype(BF16)
    r1 = x - hi.astype(F32)
    mid = r1.astype(BF16)
    lo = (r1 - mid.astype(F32)).astype(BF16)
    return hi, mid, lo


def _dot_x_exact(x, e):
    hi, mid, lo = _split3(x)
    f = lambda p: jnp.dot(p, e, preferred_element_type=F32)
    return f(hi) + f(mid) + f(lo)


def _head_sums(x, down, up):
    def two_pass(y, e):
        hi = y.astype(BF16)
        mid = (y - hi.astype(F32)).astype(BF16)
        return jnp.dot(hi, e, preferred_element_type=F32) + jnp.dot(mid, e, preferred_element_type=F32)
    return two_pass(two_pass(x, down), up)


def _dot_exact_x(e, x):
    hi, mid, lo = _split3(x)
    f = lambda p: jnp.dot(e, p, preferred_element_type=F32)
    return f(hi) + f(mid) + f(lo)


def _sigmoid(x):
    return 1.0 / (1.0 + jnp.exp(-x))


def _silu(x):
    return x * _sigmoid(x)


def _rms(x, g):
    return x * lax.rsqrt(jnp.mean(x * x, axis=-1, keepdims=True) + RMS_EPS) * g


def _modulated(x, mod_ref, g_ref):
    return _rms(x, g_ref[...]) * (1.0 + mod_ref[0, 1:2, :]) + mod_ref[0, 0:1, :]


def _const_spec(shape):
    return pl.BlockSpec(shape, lambda *_: (0,) * len(shape), pipeline_mode=pl.Buffered(1))


def _tok_spec(tt, d):
    return pl.BlockSpec((1, tt, d), lambda b, t: (b, t, 0))


def _mod_spec(d):
    return pl.BlockSpec((1, 3, d), lambda b, t: (b, 0, 0))


def _mod_kernel(c_ref, w_ref, b_ref, o_ref):
    o_ref[0] = _dot(_silu(c_ref[...]), w_ref[0]) + b_ref[0]


def _modulation(c_all, mod_w, mod_b):
    depth, d, d3 = mod_w.shape
    rows = c_all.shape[0]
    return pl.pallas_call(
        _mod_kernel,
        grid=(depth, d3 // d),
        in_specs=[pl.BlockSpec((rows, d), lambda i, n: (0, 0)),
                  pl.BlockSpec((1, d, d), lambda i, n: (i, 0, n)),
                  pl.BlockSpec((1, 1, d), lambda i, n: (i, 0, n))],
        out_specs=pl.BlockSpec((1, rows, d), lambda i, n: (i, 0, n)),
        out_shape=jax.ShapeDtypeStruct((depth, rows, d3), F32),
        name="modulation",
    )(c_all, mod_w, mod_b.reshape(depth, 1, d3))


def _rwkv_proj_kernel(x_ref, xp_ref, xn_ref, mod_ref, g_ref, w_ref, w1_ref, w2_ref, a1_ref, a2_ref,
                      mu_rkv_ref, mu_wa_ref, w0_ref, a0_ref, kk_ref, ka_ref, rk_ref, down_ref, up_ref,
                      r_out, k_out, v_out, kkn_out, sza_out, bonus_out, lw_out, a_out,
                      hext_ref, pext_ref):
    t = pl.program_id(1)
    nt = pl.num_programs(1)
    tt, d = x_ref.shape[1], x_ref.shape[2]
    h = _modulated(x_ref[0], mod_ref, g_ref)
    hext_ref[0:HALO] = jnp.where(t > 0, _modulated(xp_ref[0], mod_ref, g_ref), 0.0)
    hext_ref[HALO:HALO + tt] = h
    hext_ref[HALO + tt:] = jnp.where(t < nt - 1, _modulated(xn_ref[0], mod_ref, g_ref), 0.0)

    def shifted(ref):
        return 0.5 * (ref[pl.ds(HALO - 1, tt), :] + ref[pl.ds(HALO + 1, tt), :])

    dh = shifted(hext_ref) - h
    lane = lax.broadcasted_iota(jnp.int32, (tt, 2 * LORA), 1)

    def lora(xin, down_ref, up_ref, act):
        mid = act(_dot(xin, down_ref[...]))
        up = up_ref[...]
        return (_dot(jnp.where(lane < LORA, mid, 0.0), up), _dot(jnp.where(lane >= LORA, mid, 0.0), up))

    lw = lora(h + dh * mu_wa_ref[0:1, :], w1_ref, w2_ref, jnp.tanh)
    la = lora(h + dh * mu_wa_ref[1:2, :], a1_ref, a2_ref, lambda z: z)
    a_dir = []
    for z in range(2):
        lw_out[z, 0] = -jnp.exp(-0.5) * _sigmoid(w0_ref[z:z + 1, :] + lw[z])
        a_z = _sigmoid(a0_ref[z:z + 1, :] + la[z])
        a_out[z, 0] = a_z
        a_dir.append(a_z)

    hext_b = hext_ref[...].astype(BF16)
    hb = h.astype(BF16)
    mixed = []
    for g in range(3):
        pext_ref[...] = jnp.dot(hext_b, w_ref[:, g * d:(g + 1) * d], preferred_element_type=F32)
        centre = pext_ref[pl.ds(HALO, tt), :]
        mixed.append(centre + (shifted(pext_ref) - centre) * mu_rkv_ref[g:g + 1, :])
    r, k, v = mixed
    r_out[0] = r
    k_out[0] = k
    v_out[0] = v
    sza_out[0] = _silu(jnp.dot(hb, w_ref[:, 3 * d:4 * d], preferred_element_type=F32))

    down, up = down_ref[...], up_ref[...]
    kq = k * kk_ref[...]
    kkn_out[0] = kq * lax.rsqrt(jnp.maximum(_head_sums(kq * kq, down, up), 1e-24))
    ka = ka_ref[...]
    bsum = r * k * ((1.0 + (a_dir[0] - 1.0) * ka) * rk_ref[0:1, :] + (1.0 + (a_dir[1] - 1.0) * ka) * rk_ref[1:2, :])
    bonus_out[0] = _head_sums(bsum, down, up) * v


def _rwkv_proj(x, mod, g_pre, w_rkvz, w1c, w2c, a1c, a2c, mu_rkv, mu_wa, w0, a0, k_k, k_a, r_k, down, up, tt):
    b, t, d = x.shape
    nh = t // HALO
    step = tt // HALO
    vec = lambda n: _const_spec((n, d))
    tok = _tok_spec(tt, d)
    tok2 = pl.BlockSpec((2, 1, tt, d), lambda bi, ti: (0, bi, ti, 0))
    sd = jax.ShapeDtypeStruct((b, t, d), F32)
    sd2 = jax.ShapeDtypeStruct((2, b, t, d), F32)
    return pl.pallas_call(
        _rwkv_proj_kernel,
        grid=(b, t // tt),
        in_specs=[tok,
                  pl.BlockSpec((1, HALO, d), lambda bi, ti: (bi, jnp.maximum(ti * step - 1, 0), 0)),
                  pl.BlockSpec((1, HALO, d), lambda bi, ti: (bi, jnp.minimum((ti + 1) * step, nh - 1), 0)),
                  _mod_spec(d), vec(1),
                  _const_spec((d, 4 * d)), _const_spec((d, 2 * LORA)), _const_spec((2 * LORA, d)),
                  _const_spec((d, 2 * LORA)), _const_spec((2 * LORA, d)),
                  vec(3), vec(2), vec(2), vec(2), vec(1), vec(1), vec(2),
                  _const_spec(down.shape), _const_spec(up.shape)],
        out_specs=[tok, tok, tok, tok, tok, tok, tok2, tok2],
        out_shape=[sd, sd, sd, sd, sd, sd, sd2, sd2],
        scratch_shapes=[pltpu.VMEM((tt + 2 * HALO, d), F32), pltpu.VMEM((tt + 2 * HALO, d), F32)],
        compiler_params=pltpu.CompilerParams(dimension_semantics=("parallel", "parallel"),
                                             vmem_limit_bytes=VMEM_LIMIT),
        name="rwkv_proj",
    )(x, x, x, mod, g_pre, w_rkvz, w1c, w2c, a1c, a2c, mu_rkv, mu_wa, w0, a0, k_k, k_a, r_k, down, up)


def _sconv_kernel(x_ref, mod_ref, g_ref, w_ref, scw_ref, o_ref, uext_ref, *, seg):
    tt, d = x_ref.shape[1], x_ref.shape[2]
    hb = _modulated(x_ref[0], mod_ref, g_ref).astype(BF16)
    proj = lambda i: jnp.dot(hb, w_ref[:, i * d:(i + 1) * d], preferred_element_type=F32)
    u = proj(1) * proj(2)
    zero = jnp.zeros((HALO, d), F32)
    uext_ref[0:HALO] = zero
    uext_ref[HALO:HALO + tt] = u
    uext_ref[HALO + tt:] = zero
    pos = lax.broadcasted_iota(jnp.int32, (tt, d), 0) & (seg - 1)
    prev = jnp.where(pos > 0, uext_ref[pl.ds(HALO - 1, tt), :], 0.0)
    nxt = jnp.where(pos < seg - 1, uext_ref[pl.ds(HALO + 1, tt), :], 0.0)
    conv = scw_ref[0:1, :] * prev + scw_ref[1:2, :] * u + scw_ref[2:3, :] * nxt
    o_ref[0] = proj(0) * conv * _silu(proj(3))


def _sconv(x, mod, g_pre, w_sc, sc_w, seg, tt):
    b, t, d = x.shape
    return pl.pallas_call(
        functools.partial(_sconv_kernel, seg=seg),
        grid=(b, t // tt),
        in_specs=[_tok_spec(tt, d), _mod_spec(d), _const_spec((1, d)), _const_spec((d, 4 * d)),
                  _const_spec((3, d))],
        out_specs=_tok_spec(tt, d),
        out_shape=jax.ShapeDtypeStruct((b, t, d), F32),
        scratch_shapes=[pltpu.VMEM((tt + 2 * HALO, d), F32)],
        compiler_params=pltpu.CompilerParams(dimension_semantics=("parallel", "parallel"),
                                             vmem_limit_bytes=VMEM_LIMIT),
        name="sconv",
    )(x, mod, g_pre, w_sc, sc_w)


def _each(f, *lists):
    return [f(*xs) for xs in zip(*lists)]


def _unit_triangular_inverses(mats, blk16):
    n = mats[0].shape[0]
    eye = (lax.broadcasted_iota(jnp.int32, (n, n), 0) == lax.broadcasted_iota(jnp.int32, (n, n), 1)).astype(F32)
    dg = [jnp.where(blk16, a, 0.0) for a in mats]
    off = _each(lambda a, d: a - d, mats, dg)
    d2 = _each(_dot, dg, dg)
    d4 = _each(_dot, d2, d2)
    d8 = _each(_dot, d4, d4)
    grow = lambda tm, p: tm + _dot(tm, p)
    tm = [eye + d for d in dg]
    tm = _each(grow, tm, d2)
    tm = _each(grow, tm, d4)
    tm = _each(grow, tm, d8)
    nb = _each(_dot, tm, off)
    nb2 = _each(_dot, nb, nb)
    nb3 = _each(_dot, nb, nb2)
    return _each(lambda x, y, z, tmi: _dot(eye + x + y + z, tmi), nb, nb2, nb3, tm)


def _wkv_kernel(s0_ref, r_ref, k_ref, v_ref, kk_ref, lw_ref, a_ref, ka_ref, o_ref, s_out, s_ref, *, reverse):
    t = pl.program_id(2)
    nt = pl.num_programs(2)
    tt, width = r_ref.shape[1], r_ref.shape[2]
    nc, npair = tt // CHUNK, width // PAIR

    @pl.when(t == 0)
    def _():
        s_ref[...] = s0_ref[0]

    ri = lax.broadcasted_iota(jnp.int32, (PAIR, PAIR), 0)
    ci = lax.broadcasted_iota(jnp.int32, (PAIR, PAIR), 1)
    same = (ri >> 6) == (ci >> 6)
    blk16 = (ri >> 4) == (ci >> 4)
    rt, ct = lax.broadcasted_iota(jnp.int32, (tt, tt), 0), lax.broadcasted_iota(jnp.int32, (tt, tt), 1)
    same_chunk = (rt >> 6) == (ct >> 6)
    if reverse:
        strict, incl, tri = same & (ri < ci), same & (ri <= ci), (same_chunk & (rt <= ct)).astype(BF16)
    else:
        strict, incl, tri = same & (ri > ci), same & (ri >= ci), (same_chunk & (rt >= ct)).astype(BF16)
    first_head = lax.broadcasted_iota(jnp.int32, (CHUNK, PAIR), 1) < HEAD
    stack = lambda z: jnp.concatenate([jnp.where(first_head, z, 0.0), jnp.where(first_head, 0.0, z)], axis=0)

    lw, a, kk = lw_ref[0, 0], a_ref[0, 0], kk_ref[0]
    cum = _dot_exact_x(tri, lw)
    g_inv = jnp.exp(-cum)
    a_t = -kk * jnp.exp(cum - lw)
    r_t = r_ref[0] * jnp.exp(cum)
    b_t = kk * a * g_inv
    k_t = k_ref[0] * (1.0 + (a - 1.0) * ka_ref[...]) * g_inv
    v_t = v_ref[0]

    order = list(range(nc - 1, -1, -1) if reverse else range(nc))
    units = [(p, j) for j in order for p in range(npair)]
    cut = lambda z, p, j: z[j * CHUNK:(j + 1) * CHUNK, p * PAIR:(p + 1) * PAIR]
    a2 = [stack(cut(a_t, p, j)) for p, j in units]
    r2 = [stack(cut(r_t, p, j)) for p, j in units]
    b2 = [stack(cut(b_t, p, j)) for p, j in units]
    k2 = [stack(cut(k_t, p, j)) for p, j in units]
    v2 = [stack(cut(v_t, p, j)) for p, j in units]
    end_row = lambda j: j * CHUNK if reverse else (j + 1) * CHUNK - 1
    g_end = [jnp.exp(cum[end_row(j):end_row(j) + 1, p * PAIR:(p + 1) * PAIR]) for p, j in units]
    ar = _each(lambda x, y: jnp.concatenate([x, y], axis=0).astype(BF16), a2, r2)
    bk = _each(lambda x, y: jnp.concatenate([x, y], axis=0), b2, k2)
    sc = _each(_dot_nt, ar, bk)
    a_ab = [jnp.where(strict, x[0:PAIR, 0:PAIR], 0.0) for x in sc]
    a_ak = [jnp.where(strict, x[0:PAIR, PAIR:], 0.0) for x in sc]
    a_rbk = [jnp.concatenate([jnp.where(incl, x[PAIR:, 0:PAIR], 0.0), jnp.where(incl, x[PAIR:, PAIR:], 0.0)],
                             axis=1).astype(BF16) for x in sc]
    tm = _unit_triangular_inverses(a_ab, blk16)
    w_ak = _each(_dot, a_ak, v2)
    pre = _each(lambda m, x, w: _dot(m, jnp.concatenate([x, w], axis=1)), tm, a2, w_ak)
    reader = _each(lambda x, y: jnp.concatenate([x[:, 0:PAIR], y], axis=0).astype(BF16), pre, r2)
    bkg = _each(lambda x, g: (x * g).astype(BF16), bk, g_end)

    s = [s_ref[p] for p in range(npair)]
    for i, (p, j) in enumerate(units):
        sh = _dot_nt(reader[i], s[p])
        uv = jnp.concatenate([sh[0:PAIR] + pre[i][:, PAIR:], v2[i]], axis=0).astype(BF16)
        o2 = sh[PAIR:] + jnp.dot(a_rbk[i], uv, preferred_element_type=F32)
        o_ref[0, j * CHUNK:(j + 1) * CHUNK, p * PAIR:(p + 1) * PAIR] = o2[0:CHUNK] + o2[CHUNK:]
        s[p] = s[p] * g_end[i] + _dot_tn(uv, bkg[i])
    for p in range(npair):
        s_ref[p] = s[p]

    @pl.when(t == nt - 1)
    def _():
        for p in range(npair):
            s_out[0, p] = s[p]


def _wkv(s0, r, k, v, kk, lw, a, k_a, reverse, tt):
    b, t, d = r.shape
    nt = t // tt
    z = 1 if reverse else 0
    npair = min(SCAN_PAIRS, d // PAIR)
    width = npair * PAIR
    tpos = (lambda ti: nt - 1 - ti) if reverse else (lambda ti: ti)
    tok = pl.BlockSpec((1, tt, width), lambda bi, p, ti: (bi, tpos(ti), p))
    tokz = pl.BlockSpec((1, 1, tt, width), lambda bi, p, ti: (z, bi, tpos(ti), p))
    st = pl.BlockSpec((1, npair, PAIR, PAIR), lambda bi, p, ti: (bi, p, 0, 0))
    return pl.pallas_call(
        functools.partial(_wkv_kernel, reverse=reverse),
        grid=(b, d // width, nt),
        in_specs=[st, tok, tok, tok, tok, tokz, tokz, pl.BlockSpec((1, width), lambda bi, p, ti: (0, p))],
        out_specs=[tok, st],
        out_shape=[jax.ShapeDtypeStruct((b, t, d), F32), jax.ShapeDtypeStruct((b, d // PAIR, PAIR, PAIR), F32)],
        scratch_shapes=[pltpu.VMEM((npair, PAIR, PAIR), F32)],
        compiler_params=pltpu.CompilerParams(dimension_semantics=("parallel", "parallel", "arbitrary"),
                                             vmem_limit_bytes=VMEM_LIMIT),
        name="wkv_bwd" if reverse else "wkv_fwd",
    )(s0, r, k, v, kk, lw, a, k_a)


def _even_out_kernel(x_ref, of_ref, ob_ref, bonus_ref, sza_ref, yb_ref, mod_ref, lnw_ref, lnb_ref, gp_ref,
                     w_ref, down_ref, up_ref, o_ref):
    d = x_ref.shape[2]
    down, up = down_ref[...], up_ref[...]
    o = of_ref[0] + ob_ref[0]
    dev = o - _head_sums(o, down, up) * (1.0 / HEAD)
    var = _head_sums(dev * dev, down, up) * (1.0 / HEAD)
    ya = (dev * lax.rsqrt(var + GN_EPS) * lnw_ref[...] + lnb_ref[...] + bonus_ref[0]) * sza_ref[0]
    y = _dot(ya, w_ref[0:d, :]) + _dot(yb_ref[0], w_ref[d:2 * d, :])
    o_ref[0] = x_ref[0] + mod_ref[0, 2:3, :] * _rms(y, gp_ref[...])


def _even_out(x, o_f, o_b, bonus, sza, ybg, mod, lnx_w, lnx_b, g_post, w_out, down, up, tt):
    b, t, d = x.shape
    tok = _tok_spec(tt, d)
    vec = _const_spec((1, d))
    return pl.pallas_call(
        _even_out_kernel,
        grid=(b, t // tt),
        in_specs=[tok, tok, tok, tok, tok, tok, _mod_spec(d), vec, vec, vec,
                  _const_spec((2 * d, d)), _const_spec(down.shape), _const_spec(up.shape)],
        out_specs=tok,
        out_shape=jax.ShapeDtypeStruct((b, t, d), F32),
        compiler_params=pltpu.CompilerParams(dimension_semantics=("parallel", "parallel"),
                                             vmem_limit_bytes=VMEM_LIMIT),
        name="even_out",
    )(x, o_f, o_b, bonus, sza, ybg, mod, lnx_w, lnx_b, g_post, w_out, down, up)


def _conformer_kernel(x_ref, mod_ref, g_ref, w_in_ref, dww_ref, dwb_ref, lnw_ref, lnb_ref, w_out_ref, gp_ref,
                      o_ref, pad_ref, conv_ref, shift_ref, *, seg):
    tt, d = x_ref.shape[1], x_ref.shape[2]
    e = dww_ref.shape[1]
    taps = dww_ref.shape[0]
    span = seg + 2 * CONV_PAD
    x = x_ref[0]
    hb = _modulated(x, mod_ref, g_ref).astype(BF16)
    proj = lambda i: jnp.dot(hb, w_in_ref[:, i * e:(i + 1) * e], preferred_element_type=F32)
    glu = proj(0) * _sigmoid(proj(1))
    zero = jnp.zeros((CONV_PAD, e), F32)
    for s in range(tt // seg):
        pad_ref[s * span:s * span + CONV_PAD] = zero
        pad_ref[s * span + CONV_PAD:s * span + CONV_PAD + seg] = glu[s * seg:(s + 1) * seg]
        pad_ref[s * span + CONV_PAD + seg:(s + 1) * span] = zero

    rows = min(seg, CHUNK)
    lead = CONV_PAD - taps // 2
    win = shift_ref.shape[1]
    for s in range(tt // seg):
        for r0 in range(0, seg, rows):
            for c0 in range(0, e, CONV_COLS):
                cols = pl.ds(c0, CONV_COLS)
                for phase in range(8):
                    shift_ref[phase] = pad_ref[pl.ds(s * span + r0 + phase, win), cols]
                acc = jnp.zeros((rows, CONV_COLS), F32)
                for j in range(taps):
                    q, phase = divmod(lead + j, 8)
                    acc = acc + dww_ref[j:j + 1, cols] * shift_ref[phase, pl.ds(8 * q, rows), :]
                conv_ref[pl.ds(s * seg + r0, rows), cols] = acc

    conv = conv_ref[...] + dwb_ref[...]
    dev = conv - jnp.mean(conv, axis=-1, keepdims=True)
    var = jnp.mean(dev * dev, axis=-1, keepdims=True)
    act = _silu(dev * lax.rsqrt(var + LN_EPS) * lnw_ref[...] + lnb_ref[...]) * _silu(proj(2))
    o_ref[0] = x + mod_ref[0, 2:3, :] * _rms(_dot(act, w_out_ref[...]), gp_ref[...])


def _conformer(x, mod, g_pre, w_in, dw_w, dw_b, ln_w, ln_b, w_out, g_post, seg, tt):
    b, t, d = x.shape
    taps, e = dw_w.shape
    assert taps // 2 <= CONV_PAD and tt % seg == 0
    rows = min(seg, CHUNK)
    win = rows + 8 * ((CONV_PAD + taps // 2) // 8)
    assert 7 + win <= rows + 2 * CONV_PAD
    tok = _tok_spec(tt, d)
    return pl.pallas_call(
        functools.partial(_conformer_kernel, seg=seg),
        grid=(b, t // tt),
        in_specs=[tok, _mod_spec(d), _const_spec((1, d)), _const_spec((d, 3 * e)), _const_spec((taps, e)),
                  _const_spec((1, e)), _const_spec((1, e)), _const_spec((1, e)), _const_spec((e, d)),
                  _const_spec((1, d))],
        out_specs=tok,
        out_shape=jax.ShapeDtypeStruct((b, t, d), F32),
        scratch_shapes=[pltpu.VMEM(((tt // seg) * (seg + 2 * CONV_PAD), e), F32), pltpu.VMEM((tt, e), F32),
                        pltpu.VMEM((8, win, CONV_COLS), F32)],
        compiler_params=pltpu.CompilerParams(dimension_semantics=("parallel", "parallel"),
                                             vmem_limit_bytes=VMEM_LIMIT),
        name="conformer",
    )(x, mod, g_pre, w_in, dw_w, dw_b, ln_w, ln_b, w_out, g_post)


def _to_col_major(h, rows):
    b, t, ch = h.shape
    return h.reshape(b, rows, GRID_W, ch).swapaxes(1, 2).reshape(b, t, ch)


def _to_row_major(h, rows):
    b, t, ch = h.shape
    return h.reshape(b, GRID_W, rows, ch).swapaxes(1, 2).reshape(b, t, ch)


def kernel(x, c, ctx, c_ctx, mod_w, mod_b, g_pre, g_post, ev_w_in, ev_w_out, ev_mu_rkv, ev_mu_wa, ev_w0, ev_w1, ev_w2, ev_a0, ev_a1, ev_a2, ev_k_k, ev_k_a, ev_r_k, ev_lnx_w, ev_lnx_b, ev_sc_w, od_w_in, od_dw_w, od_dw_b, od_ln_w, od_ln_b, od_w_out):
    b, t, d = x.shape
    n_ctx = ctx.shape[1]
    depth = mod_w.shape[0]
    rows = t // GRID_W
    assert d % PAIR == 0 and t % TOKEN_TILE == 0 and n_ctx % CHUNK == 0
    tt_ctx = min(TOKEN_TILE, n_ctx)

    mod_rows = -(-(b + 1) // 8) * 8
    c_all = jnp.zeros((mod_rows, d), F32).at[:b].set(c).at[b].set(c_ctx)
    mods = _modulation(c_all, mod_w, mod_b).reshape(depth, mod_rows, 3, d)
    head_cols = -(-(d // HEAD) // 128) * 128
    down = (jnp.arange(d)[:, None] // HEAD == jnp.arange(head_cols)[None, :]).astype(BF16)
    sums = (down, down.T)

    col_major = False
    for i in range(depth):
        j = i // 2
        last = i == depth - 1
        col = j % 2 == 1
        if col != col_major:
            x = _to_col_major(x, rows) if col else _to_row_major(x, rows)
            col_major = col
        seg = rows if col else GRID_W
        mod_x = mods[i, :b]
        mod_c = jnp.broadcast_to(mods[i, b], (b, 3, d))
        gpre, gpost = g_pre[i][None], g_post[i][None]
        if i % 2 == 0:
            w_in = ev_w_in[j].astype(BF16)
            w_rkvz, w_sc = w_in[:, :4 * d], w_in[:, 4 * d:]
            w_out = ev_w_out[j].astype(BF16)
            cat = lambda w: jnp.concatenate([w[0], w[1]], axis=1).astype(BF16)
            stk = lambda w: jnp.concatenate([w[0], w[1]], axis=0).astype(BF16)
            k_a = ev_k_a[j][None]
            proj_args = (gpre, w_rkvz, cat(ev_w1[j]), stk(ev_w2[j]), cat(ev_a1[j]), stk(ev_a2[j]), ev_mu_rkv[j],
                         ev_mu_wa[j], ev_w0[j], ev_a0[j], ev_k_k[j][None], k_a, ev_r_k[j].reshape(2, d), *sums)
            pc = _rwkv_proj(ctx, mod_c, *proj_args, tt_ctx)
            px = _rwkv_proj(x, mod_x, *proj_args, TOKEN_TILE)
            zero = jnp.zeros((b, d // PAIR, PAIR, PAIR), F32)
            scan = lambda p, s0, rev, tt: _wkv(s0, p[0], p[1], p[2], p[3], p[6], p[7], k_a, rev, tt)
            oc_f, s_f = scan(pc, zero, False, min(SCAN_TILE, n_ctx))
            oc_b, s_b = scan(pc, zero, True, min(SCAN_TILE, n_ctx))
            ox_f, _ = scan(px, s_f, False, SCAN_TILE)
            ox_b, _ = scan(px, s_b, True, SCAN_TILE)
            yb_x = _sconv(x, mod_x, gpre, w_sc, ev_sc_w[j], seg, TOKEN_TILE)
            out_args = (ev_lnx_w[j][None], ev_lnx_b[j][None], gpost, w_out, *sums)
            x = _even_out(x, ox_f, ox_b, px[5], px[4], yb_x, mod_x, *out_args, TOKEN_TILE)
            if not last:
                yb_c = _sconv(ctx, mod_c, gpre, w_sc, ev_sc_w[j], n_ctx, tt_ctx)
                ctx = _even_out(ctx, oc_f, oc_b, pc[5], pc[4], yb_c, mod_c, *out_args, tt_ctx)
        else:
            odd_args = (gpre, od_w_in[j].astype(BF16), od_dw_w[j], od_dw_b[j][None], od_ln_w[j][None],
                        od_ln_b[j][None], od_w_out[j].astype(BF16), gpost)
            if not last:
                ctx_new = _conformer(ctx, mod_c, *odd_args, n_ctx, tt_ctx)
            x = _conformer(x, mod_x, *odd_args, seg, TOKEN_TILE)
            if not last:
                ctx = ctx_new
    if col_major:
        x = _to_row_major(x, rows)
    return x
```

```python
import functools

import jax
import jax.numpy as jnp
from jax import lax
from jax.experimental import pallas as pl
from jax.experimental.pallas import tpu as pltpu

F32 = jnp.float32
BF16 = jnp.bfloat16

GRID_W = 64
HEAD = 64
PAIR = 2 * HEAD
CHUNK = 64
LORA = 64
RMS_EPS = 1e-6
LN_EPS = 1e-5
GN_EPS = 64e-5
HALO = 8
TOKEN_TILE = 256
SCAN_TILE = 256
SCAN_PAIRS = 4
CONV_PAD = 16
CONV_COLS = 512
VMEM_LIMIT = 56 * 1024 * 1024


def _dot(a, b):
    return jnp.dot(a.astype(BF16), b.astype(BF16), preferred_element_type=F32)


def _dot_nt(a, b):
    return lax.dot_general(a.astype(BF16), b.astype(BF16), (((1,), (1,)), ((), ())),
                           preferred_element_type=F32)


def _dot_tn(a, b):
    return lax.dot_general(a.astype(BF16), b.astype(BF16), (((0,), (0,)), ((), ())),
                           preferred_element_type=F32)


def _split3(x):
    hi = x.astype(BF16)
    r1 = x - hi.astype(F32)
    mid = r1.astype(BF16)
    lo = (r1 - mid.astype(F32)).astype(BF16)
    return hi, mid, lo


def _dot_x_exact(x, e):
    hi, mid, lo = _split3(x)
    f = lambda p: jnp.dot(p, e, preferred_element_type=F32)
    return f(hi) + f(mid) + f(lo)


def _head_sums(x, down, up):
    def two_pass(y, e):
        hi = y.astype(BF16)
        mid = (y - hi.astype(F32)).astype(BF16)
        return jnp.dot(hi, e, preferred_element_type=F32) + jnp.dot(mid, e, preferred_element_type=F32)
    return two_pass(two_pass(x, down), up)


def _dot_exact_x(e, x):
    hi, mid, lo = _split3(x)
    f = lambda p: jnp.dot(e, p, preferred_element_type=F32)
    return f(hi) + f(mid) + f(lo)


def _sigmoid(x):
    return 1.0 / (1.0 + jnp.exp(-x))


def _silu(x):
    return x * _sigmoid(x)


def _rms(x, g):
    return x * lax.rsqrt(jnp.mean(x * x, axis=-1, keepdims=True) + RMS_EPS) * g


def _modulated(x, mod_ref, g_ref):
    return _rms(x, g_ref[...]) * (1.0 + mod_ref[0, 1:2, :]) + mod_ref[0, 0:1, :]


def _const_spec(shape):
    return pl.BlockSpec(shape, lambda *_: (0,) * len(shape), pipeline_mode=pl.Buffered(1))


def _tok_spec(tt, d):
    return pl.BlockSpec((1, tt, d), lambda b, t: (b, t, 0))


def _mod_spec(d):
    return pl.BlockSpec((1, 3, d), lambda b, t: (b, 0, 0))


def _mod_kernel(c_ref, w_ref, b_ref, o_ref):
    o_ref[0] = _dot(_silu(c_ref[...]), w_ref[0]) + b_ref[0]


def _modulation(c_all, mod_w, mod_b):
    depth, d, d3 = mod_w.shape
    rows = c_all.shape[0]
    return pl.pallas_call(
        _mod_kernel,
        grid=(depth, d3 // d),
        in_specs=[pl.BlockSpec((rows, d), lambda i, n: (0, 0)),
                  pl.BlockSpec((1, d, d), lambda i, n: (i, 0, n)),
                  pl.BlockSpec((1, 1, d), lambda i, n: (i, 0, n))],
        out_specs=pl.BlockSpec((1, rows, d), lambda i, n: (i, 0, n)),
        out_shape=jax.ShapeDtypeStruct((depth, rows, d3), F32),
        name="modulation",
    )(c_all, mod_w, mod_b.reshape(depth, 1, d3))


def _rwkv_proj_kernel(x_ref, xp_ref, xn_ref, mod_ref, g_ref, w_ref, w1_ref, w2_ref, a1_ref, a2_ref,
                      mu_rkv_ref, mu_wa_ref, w0_ref, a0_ref, kk_ref, ka_ref, rk_ref, down_ref, up_ref,
                      r_out, k_out, v_out, kkn_out, sza_out, bonus_out, lw_out, a_out,
                      hext_ref, pext_ref):
    t = pl.program_id(1)
    nt = pl.num_programs(1)
    tt, d = x_ref.shape[1], x_ref.shape[2]
    h = _modulated(x_ref[0], mod_ref, g_ref)
    hext_ref[0:HALO] = jnp.where(t > 0, _modulated(xp_ref[0], mod_ref, g_ref), 0.0)
    hext_ref[HALO:HALO + tt] = h
    hext_ref[HALO + tt:] = jnp.where(t < nt - 1, _modulated(xn_ref[0], mod_ref, g_ref), 0.0)

    def shifted(ref):
        return 0.5 * (ref[pl.ds(HALO - 1, tt), :] + ref[pl.ds(HALO + 1, tt), :])

    dh = shifted(hext_ref) - h
    lane = lax.broadcasted_iota(jnp.int32, (tt, 2 * LORA), 1)

    def lora(xin, down_ref, up_ref, act):
        mid = act(_dot(xin, down_ref[...]))
        up = up_ref[...]
        return (_dot(jnp.where(lane < LORA, mid, 0.0), up), _dot(jnp.where(lane >= LORA, mid, 0.0), up))

    lw = lora(h + dh * mu_wa_ref[0:1, :], w1_ref, w2_ref, jnp.tanh)
    la = lora(h + dh * mu_wa_ref[1:2, :], a1_ref, a2_ref, lambda z: z)
    a_dir = []
    for z in range(2):
        lw_out[z, 0] = -jnp.exp(-0.5) * _sigmoid(w0_ref[z:z + 1, :] + lw[z])
        a_z = _sigmoid(a0_ref[z:z + 1, :] + la[z])
        a_out[z, 0] = a_z
        a_dir.append(a_z)

    hext_b = hext_ref[...].astype(BF16)
    hb = h.astype(BF16)
    mixed = []
    for g in range(3):
        pext_ref[...] = jnp.dot(hext_b, w_ref[:, g * d:(g + 1) * d], preferred_element_type=F32)
        centre = pext_ref[pl.ds(HALO, tt), :]
        mixed.append(centre + (shifted(pext_ref) - centre) * mu_rkv_ref[g:g + 1, :])
    r, k, v = mixed
    r_out[0] = r
    k_out[0] = k
    v_out[0] = v
    sza_out[0] = _silu(jnp.dot(hb, w_ref[:, 3 * d:4 * d], preferred_element_type=F32))

    down, up = down_ref[...], up_ref[...]
    kq = k * kk_ref[...]
    kkn_out[0] = kq * lax.rsqrt(jnp.maximum(_head_sums(kq * kq, down, up), 1e-24))
    ka = ka_ref[...]
    bsum = r * k * ((1.0 + (a_dir[0] - 1.0) * ka) * rk_ref[0:1, :] + (1.0 + (a_dir[1] - 1.0) * ka) * rk_ref[1:2, :])
    bonus_out[0] = _head_sums(bsum, down, up) * v


def _rwkv_proj(x, mod, g_pre, w_rkvz, w1c, w2c, a1c, a2c, mu_rkv, mu_wa, w0, a0, k_k, k_a, r_k, down, up, tt):
    b, t, d = x.shape
    nh = t // HALO
    step = tt // HALO
    vec = lambda n: _const_spec((n, d))
    tok = _tok_spec(tt, d)
    tok2 = pl.BlockSpec((2, 1, tt, d), lambda bi, ti: (0, bi, ti, 0))
    sd = jax.ShapeDtypeStruct((b, t, d), F32)
    sd2 = jax.ShapeDtypeStruct((2, b, t, d), F32)
    return pl.pallas_call(
        _rwkv_proj_kernel,
        grid=(b, t // tt),
        in_specs=[tok,
                  pl.BlockSpec((1, HALO, d), lambda bi, ti: (bi, jnp.maximum(ti * step - 1, 0), 0)),
                  pl.BlockSpec((1, HALO, d), lambda bi, ti: (bi, jnp.minimum((ti + 1) * step, nh - 1), 0)),
                  _mod_spec(d), vec(1),
                  _const_spec((d, 4 * d)), _const_spec((d, 2 * LORA)), _const_spec((2 * LORA, d)),
                  _const_spec((d, 2 * LORA)), _const_spec((2 * LORA, d)),
                  vec(3), vec(2), vec(2), vec(2), vec(1), vec(1), vec(2),
                  _const_spec(down.shape), _const_spec(up.shape)],
        out_specs=[tok, tok, tok, tok, tok, tok, tok2, tok2],
        out_shape=[sd, sd, sd, sd, sd, sd, sd2, sd2],
        scratch_shapes=[pltpu.VMEM((tt + 2 * HALO, d), F32), pltpu.VMEM((tt + 2 * HALO, d), F32)],
        compiler_params=pltpu.CompilerParams(dimension_semantics=("parallel", "parallel"),
                                             vmem_limit_bytes=VMEM_LIMIT),
        name="rwkv_proj",
    )(x, x, x, mod, g_pre, w_rkvz, w1c, w2c, a1c, a2c, mu_rkv, mu_wa, w0, a0, k_k, k_a, r_k, down, up)


def _sconv_kernel(x_ref, mod_ref, g_ref, w_ref, scw_ref, o_ref, uext_ref, *, seg):
    tt, d = x_ref.shape[1], x_ref.shape[2]
    hb = _modulated(x_ref[0], mod_ref, g_ref).astype(BF16)
    proj = lambda i: jnp.dot(hb, w_ref[:, i * d:(i + 1) * d], preferred_element_type=F32)
    u = proj(1) * proj(2)
    zero = jnp.zeros((HALO, d), F32)
    uext_ref[0:HALO] = zero
    uext_ref[HALO:HALO + tt] = u
    uext_ref[HALO + tt:] = zero
    pos = lax.broadcasted_iota(jnp.int32, (tt, d), 0) & (seg - 1)
    prev = jnp.where(pos > 0, uext_ref[pl.ds(HALO - 1, tt), :], 0.0)
    nxt = jnp.where(pos < seg - 1, uext_ref[pl.ds(HALO + 1, tt), :], 0.0)
    conv = scw_ref[0:1, :] * prev + scw_ref[1:2, :] * u + scw_ref[2:3, :] * nxt
    o_ref[0] = proj(0) * conv * _silu(proj(3))


def _sconv(x, mod, g_pre, w_sc, sc_w, seg, tt):
    b, t, d = x.shape
    return pl.pallas_call(
        functools.partial(_sconv_kernel, seg=seg),
        grid=(b, t // tt),
        in_specs=[_tok_spec(tt, d), _mod_spec(d), _const_spec((1, d)), _const_spec((d, 4 * d)),
                  _const_spec((3, d))],
        out_specs=_tok_spec(tt, d),
        out_shape=jax.ShapeDtypeStruct((b, t, d), F32),
        scratch_shapes=[pltpu.VMEM((tt + 2 * HALO, d), F32)],
        compiler_params=pltpu.CompilerParams(dimension_semantics=("parallel", "parallel"),
                                             vmem_limit_bytes=VMEM_LIMIT),
        name="sconv",
    )(x, mod, g_pre, w_sc, sc_w)


def _each(f, *lists):
    return [f(*xs) for xs in zip(*lists)]


def _unit_triangular_inverses(mats, blk16):
    n = mats[0].shape[0]
    eye = (lax.broadcasted_iota(jnp.int32, (n, n), 0) == lax.broadcasted_iota(jnp.int32, (n, n), 1)).astype(F32)
    dg = [jnp.where(blk16, a, 0.0) for a in mats]
    off = _each(lambda a, d: a - d, mats, dg)
    d2 = _each(_dot, dg, dg)
    d4 = _each(_dot, d2, d2)
    d8 = _each(_dot, d4, d4)
    grow = lambda tm, p: tm + _dot(tm, p)
    tm = [eye + d for d in dg]
    tm = _each(grow, tm, d2)
    tm = _each(grow, tm, d4)
    tm = _each(grow, tm, d8)
    nb = _each(_dot, tm, off)
    nb2 = _each(_dot, nb, nb)
    nb3 = _each(_dot, nb, nb2)
    return _each(lambda x, y, z, tmi: _dot(eye + x + y + z, tmi), nb, nb2, nb3, tm)


def _wkv_kernel(s0_ref, r_ref, k_ref, v_ref, kk_ref, lw_ref, a_ref, ka_ref, o_ref, s_out, s_ref, *, reverse):
    t = pl.program_id(2)
    nt = pl.num_programs(2)
    tt, width = r_ref.shape[1], r_ref.shape[2]
    nc, npair = tt // CHUNK, width // PAIR

    @pl.when(t == 0)
    def _():
        s_ref[...] = s0_ref[0]

    ri = lax.broadcasted_iota(jnp.int32, (PAIR, PAIR), 0)
    ci = lax.broadcasted_iota(jnp.int32, (PAIR, PAIR), 1)
    same = (ri >> 6) == (ci >> 6)
    blk16 = (ri >> 4) == (ci >> 4)
    rt, ct = lax.broadcasted_iota(jnp.int32, (tt, tt), 0), lax.broadcasted_iota(jnp.int32, (tt, tt), 1)
    same_chunk = (rt >> 6) == (ct >> 6)
    if reverse:
        strict, incl, tri = same & (ri < ci), same & (ri <= ci), (same_chunk & (rt <= ct)).astype(BF16)
    else:
        strict, incl, tri = same & (ri > ci), same & (ri >= ci), (same_chunk & (rt >= ct)).astype(BF16)
    first_head = lax.broadcasted_iota(jnp.int32, (CHUNK, PAIR), 1) < HEAD
    stack = lambda z: jnp.concatenate([jnp.where(first_head, z, 0.0), jnp.where(first_head, 0.0, z)], axis=0)

    lw, a, kk = lw_ref[0, 0], a_ref[0, 0], kk_ref[0]
    cum = _dot_exact_x(tri, lw)
    g_inv = jnp.exp(-cum)
    a_t = -kk * jnp.exp(cum - lw)
    r_t = r_ref[0] * jnp.exp(cum)
    b_t = kk * a * g_inv
    k_t = k_ref[0] * (1.0 + (a - 1.0) * ka_ref[...]) * g_inv
    v_t = v_ref[0]

    order = list(range(nc - 1, -1, -1) if reverse else range(nc))
    units = [(p, j) for j in order for p in range(npair)]
    cut = lambda z, p, j: z[j * CHUNK:(j + 1) * CHUNK, p * PAIR:(p + 1) * PAIR]
    a2 = [stack(cut(a_t, p, j)) for p, j in units]
    r2 = [stack(cut(r_t, p, j)) for p, j in units]
    b2 = [stack(cut(b_t, p, j)) for p, j in units]
    k2 = [stack(cut(k_t, p, j)) for p, j in units]
    v2 = [stack(cut(v_t, p, j)) for p, j in units]
    end_row = lambda j: j * CHUNK if reverse else (j + 1) * CHUNK - 1
    g_end = [jnp.exp(cum[end_row(j):end_row(j) + 1, p * PAIR:(p + 1) * PAIR]) for p, j in units]
    ar = _each(lambda x, y: jnp.concatenate([x, y], axis=0).astype(BF16), a2, r2)
    bk = _each(lambda x, y: jnp.concatenate([x, y], axis=0), b2, k2)
    sc = _each(_dot_nt, ar, bk)
    a_ab = [jnp.where(strict, x[0:PAIR, 0:PAIR], 0.0) for x in sc]
    a_ak = [jnp.where(strict, x[0:PAIR, PAIR:], 0.0) for x in sc]
    a_rbk = [jnp.concatenate([jnp.where(incl, x[PAIR:, 0:PAIR], 0.0), jnp.where(incl, x[PAIR:, PAIR:], 0.0)],
                             axis=1).astype(BF16) for x in sc]
    tm = _unit_triangular_inverses(a_ab, blk16)
    w_ak = _each(_dot, a_ak, v2)
    pre = _each(lambda m, x, w: _dot(m, jnp.concatenate([x, w], axis=1)), tm, a2, w_ak)
    reader = _each(lambda x, y: jnp.concatenate([x[:, 0:PAIR], y], axis=0).astype(BF16), pre, r2)
    bkg = _each(lambda x, g: (x * g).astype(BF16), bk, g_end)

    s = [s_ref[p] for p in range(npair)]
    for i, (p, j) in enumerate(units):
        sh = _dot_nt(reader[i], s[p])
        uv = jnp.concatenate([sh[0:PAIR] + pre[i][:, PAIR:], v2[i]], axis=0).astype(BF16)
        o2 = sh[PAIR:] + jnp.dot(a_rbk[i], uv, preferred_element_type=F32)
        o_ref[0, j * CHUNK:(j + 1) * CHUNK, p * PAIR:(p + 1) * PAIR] = o2[0:CHUNK] + o2[CHUNK:]
        s[p] = s[p] * g_end[i] + _dot_tn(uv, bkg[i])
    for p in range(npair):
        s_ref[p] = s[p]

    @pl.when(t == nt - 1)
    def _():
        for p in range(npair):
            s_out[0, p] = s[p]


def _wkv(s0, r, k, v, kk, lw, a, k_a, reverse, tt):
    b, t, d = r.shape
    nt = t // tt
    z = 1 if reverse else 0
    npair = min(SCAN_PAIRS, d // PAIR)
    width = npair * PAIR
    tpos = (lambda ti: nt - 1 - ti) if reverse else (lambda ti: ti)
    tok = pl.BlockSpec((1, tt, width), lambda bi, p, ti: (bi, tpos(ti), p))
    tokz = pl.BlockSpec((1, 1, tt, width), lambda bi, p, ti: (z, bi, tpos(ti), p))
    st = pl.BlockSpec((1, npair, PAIR, PAIR), lambda bi, p, ti: (bi, p, 0, 0))
    return pl.pallas_call(
        functools.partial(_wkv_kernel, reverse=reverse),
        grid=(b, d // width, nt),
        in_specs=[st, tok, tok, tok, tok, tokz, tokz, pl.BlockSpec((1, width), lambda bi, p, ti: (0, p))],
        out_specs=[tok, st],
        out_shape=[jax.ShapeDtypeStruct((b, t, d), F32), jax.ShapeDtypeStruct((b, d // PAIR, PAIR, PAIR), F32)],
        scratch_shapes=[pltpu.VMEM((npair, PAIR, PAIR), F32)],
        compiler_params=pltpu.CompilerParams(dimension_semantics=("parallel", "parallel", "arbitrary"),
                                             vmem_limit_bytes=VMEM_LIMIT),
        name="wkv_bwd" if reverse else "wkv_fwd",
    )(s0, r, k, v, kk, lw, a, k_a)


def _even_out_kernel(x_ref, of_ref, ob_ref, bonus_ref, sza_ref, yb_ref, mod_ref, lnw_ref, lnb_ref, gp_ref,
                     w_ref, down_ref, up_ref, o_ref):
    d = x_ref.shape[2]
    down, up = down_ref[...], up_ref[...]
    o = of_ref[0] + ob_ref[0]
    dev = o - _head_sums(o, down, up) * (1.0 / HEAD)
    var = _head_sums(dev * dev, down, up) * (1.0 / HEAD)
    ya = (dev * lax.rsqrt(var + GN_EPS) * lnw_ref[...] + lnb_ref[...] + bonus_ref[0]) * sza_ref[0]
    y = _dot(ya, w_ref[0:d, :]) + _dot(yb_ref[0], w_ref[d:2 * d, :])
    o_ref[0] = x_ref[0] + mod_ref[0, 2:3, :] * _rms(y, gp_ref[...])


def _even_out(x, o_f, o_b, bonus, sza, ybg, mod, lnx_w, lnx_b, g_post, w_out, down, up, tt):
    b, t, d = x.shape
    tok = _tok_spec(tt, d)
    vec = _const_spec((1, d))
    return pl.pallas_call(
        _even_out_kernel,
        grid=(b, t // tt),
        in_specs=[tok, tok, tok, tok, tok, tok, _mod_spec(d), vec, vec, vec,
                  _const_spec((2 * d, d)), _const_spec(down.shape), _const_spec(up.shape)],
        out_specs=tok,
        out_shape=jax.ShapeDtypeStruct((b, t, d), F32),
        compiler_params=pltpu.CompilerParams(dimension_semantics=("parallel", "parallel"),
                                             vmem_limit_bytes=VMEM_LIMIT),
        name="even_out",
    )(x, o_f, o_b, bonus, sza, ybg, mod, lnx_w, lnx_b, g_post, w_out, down, up)


def _conformer_kernel(x_ref, mod_ref, g_ref, w_in_ref, dww_ref, dwb_ref, lnw_ref, lnb_ref, w_out_ref, gp_ref,
                      o_ref, pad_ref, conv_ref, shift_ref, *, seg):
    tt, d = x_ref.shape[1], x_ref.shape[2]
    e = dww_ref.shape[1]
    taps = dww_ref.shape[0]
    span = seg + 2 * CONV_PAD
    x = x_ref[0]
    hb = _modulated(x, mod_ref, g_ref).astype(BF16)
    proj = lambda i: jnp.dot(hb, w_in_ref[:, i * e:(i + 1) * e], preferred_element_type=F32)
    glu = proj(0) * _sigmoid(proj(1))
    zero = jnp.zeros((CONV_PAD, e), F32)
    for s in range(tt // seg):
        pad_ref[s * span:s * span + CONV_PAD] = zero
        pad_ref[s * span + CONV_PAD:s * span + CONV_PAD + seg] = glu[s * seg:(s + 1) * seg]
        pad_ref[s * span + CONV_PAD + seg:(s + 1) * span] = zero

    rows = min(seg, CHUNK)
    lead = CONV_PAD - taps // 2
    win = shift_ref.shape[1]
    for s in range(tt // seg):
        for r0 in range(0, seg, rows):
            for c0 in range(0, e, CONV_COLS):
                cols = pl.ds(c0, CONV_COLS)
                for phase in range(8):
                    shift_ref[phase] = pad_ref[pl.ds(s * span + r0 + phase, win), cols]
                acc = jnp.zeros((rows, CONV_COLS), F32)
                for j in range(taps):
                    q, phase = divmod(lead + j, 8)
                    acc = acc + dww_ref[j:j + 1, cols] * shift_ref[phase, pl.ds(8 * q, rows), :]
                conv_ref[pl.ds(s * seg + r0, rows), cols] = acc

    conv = conv_ref[...] + dwb_ref[...]
    dev = conv - jnp.mean(conv, axis=-1, keepdims=True)
    var = jnp.mean(dev * dev, axis=-1, keepdims=True)
    act = _silu(dev * lax.rsqrt(var + LN_EPS) * lnw_ref[...] + lnb_ref[...]) * _silu(proj(2))
    o_ref[0] = x + mod_ref[0, 2:3, :] * _rms(_dot(act, w_out_ref[...]), gp_ref[...])


def _conformer(x, mod, g_pre, w_in, dw_w, dw_b, ln_w, ln_b, w_out, g_post, seg, tt):
    b, t, d = x.shape
    taps, e = dw_w.shape
    assert taps // 2 <= CONV_PAD and tt % seg == 0
    rows = min(seg, CHUNK)
    win = rows + 8 * ((CONV_PAD + taps // 2) // 8)
    assert 7 + win <= rows + 2 * CONV_PAD
    tok = _tok_spec(tt, d)
    return pl.pallas_call(
        functools.partial(_conformer_kernel, seg=seg),
        grid=(b, t // tt),
        in_specs=[tok, _mod_spec(d), _const_spec((1, d)), _const_spec((d, 3 * e)), _const_spec((taps, e)),
                  _const_spec((1, e)), _const_spec((1, e)), _const_spec((1, e)), _const_spec((e, d)),
                  _const_spec((1, d))],
        out_specs=tok,
        out_shape=jax.ShapeDtypeStruct((b, t, d), F32),
        scratch_shapes=[pltpu.VMEM(((tt // seg) * (seg + 2 * CONV_PAD), e), F32), pltpu.VMEM((tt, e), F32),
                        pltpu.VMEM((8, win, CONV_COLS), F32)],
        compiler_params=pltpu.CompilerParams(dimension_semantics=("parallel", "parallel"),
                                             vmem_limit_bytes=VMEM_LIMIT),
        name="conformer",
    )(x, mod, g_pre, w_in, dw_w, dw_b, ln_w, ln_b, w_out, g_post)


def _conformer_tm_kernel(x_ref, mod_ref, g_ref, w_in_ref, dww_ref, dwb_ref, lnw_ref, lnb_ref, w_out_ref, gp_ref,
                         o_ref, pad_ref, conv_ref, gate_ref):
    seg, nseg, d = x_ref.shape
    n = seg * nseg
    taps, e = dww_ref.shape[0], dww_ref.shape[2]
    x = x_ref[...].reshape(n, d)
    hb = _modulated(x, mod_ref, g_ref).astype(BF16)
    proj = lambda lo, hi: jnp.dot(hb, w_in_ref[:, lo:hi], preferred_element_type=F32)
    zero = jnp.zeros((CONV_PAD, nseg, e), F32)
    pad_ref[0:CONV_PAD] = zero
    pad_ref[CONV_PAD + seg:] = zero
    for c0 in range(0, e, CONV_COLS):
        glu = proj(c0, c0 + CONV_COLS) * _sigmoid(proj(e + c0, e + c0 + CONV_COLS))
        pad_ref[CONV_PAD:CONV_PAD + seg, :, c0:c0 + CONV_COLS] = glu.reshape(seg, nseg, CONV_COLS)

    lead = CONV_PAD - taps // 2

    for c0 in range(0, e, CONV_COLS):
        cols = pl.ds(c0, CONV_COLS)
        for t0 in range(0, seg, 8):
            acc = jnp.zeros((8, nseg, CONV_COLS), F32)
            for j in range(taps):
                acc = acc + dww_ref[j, :, cols][None] * pad_ref[pl.ds(t0 + lead + j, 8), :, cols]
            conv_ref[pl.ds(t0, 8), :, cols] = acc
        gate_ref[:, c0:c0 + CONV_COLS] = _silu(proj(2 * e + c0, 2 * e + c0 + CONV_COLS))

    conv = conv_ref[...].reshape(n, e) + dwb_ref[...]
    dev = conv - jnp.mean(conv, axis=-1, keepdims=True)
    var = jnp.mean(dev * dev, axis=-1, keepdims=True)
    act = _silu(dev * lax.rsqrt(var + LN_EPS) * lnw_ref[...] + lnb_ref[...]) * gate_ref[...]
    out = x + mod_ref[0, 2:3, :] * _rms(_dot(act, w_out_ref[...]), gp_ref[...])
    o_ref[...] = out.reshape(seg, nseg, d)


def _conformer_tm(x, mod, g_pre, w_in, dw_w, dw_b, ln_w, ln_b, w_out, g_post, seg):
    b, t, d = x.shape
    taps, e = dw_w.shape
    nseg = 8
    groups = t // (seg * nseg)
    assert taps // 2 <= CONV_PAD and seg % 8 == 0 and groups * seg * nseg == t
    x5 = x.reshape(b, seg, groups, nseg, d)
    tile = pl.BlockSpec((None, seg, None, nseg, d), lambda bi, gi: (bi, 0, gi, 0, 0))
    out = pl.pallas_call(
        _conformer_tm_kernel,
        grid=(b, groups),
        in_specs=[tile, _mod_spec(d), _const_spec((1, d)), _const_spec((d, 3 * e)), _const_spec((taps, nseg, e)),
                  _const_spec((1, e)), _const_spec((1, e)), _const_spec((1, e)), _const_spec((e, d)),
                  _const_spec((1, d))],
        out_specs=tile,
        out_shape=jax.ShapeDtypeStruct(x5.shape, F32),
        scratch_shapes=[pltpu.VMEM((seg + 2 * CONV_PAD, nseg, e), F32), pltpu.VMEM((seg, nseg, e), F32),
                        pltpu.VMEM((seg * nseg, e), F32)],
        compiler_params=pltpu.CompilerParams(dimension_semantics=("parallel", "parallel"),
                                             vmem_limit_bytes=VMEM_LIMIT),
        name="conformer_tm",
    )(x5, mod, g_pre, w_in, jnp.broadcast_to(dw_w[:, None, :], (taps, nseg, e)), dw_b, ln_w, ln_b, w_out, g_post)
    return out.reshape(b, t, d)


def _to_col_major(h, rows):
    b, t, ch = h.shape
    return h.reshape(b, rows, GRID_W, ch).swapaxes(1, 2).reshape(b, t, ch)


def _to_row_major(h, rows):
    b, t, ch = h.shape
    return h.reshape(b, GRID_W, rows, ch).swapaxes(1, 2).reshape(b, t, ch)


def kernel(x, c, ctx, c_ctx, mod_w, mod_b, g_pre, g_post, ev_w_in, ev_w_out, ev_mu_rkv, ev_mu_wa, ev_w0, ev_w1, ev_w2, ev_a0, ev_a1, ev_a2, ev_k_k, ev_k_a, ev_r_k, ev_lnx_w, ev_lnx_b, ev_sc_w, od_w_in, od_dw_w, od_dw_b, od_ln_w, od_ln_b, od_w_out):
    b, t, d = x.shape
    n_ctx = ctx.shape[1]
    depth = mod_w.shape[0]
    rows = t // GRID_W
    assert d % PAIR == 0 and t % TOKEN_TILE == 0 and n_ctx % CHUNK == 0
    tt_ctx = min(TOKEN_TILE, n_ctx)

    mod_rows = -(-(b + 1) // 8) * 8
    c_all = jnp.zeros((mod_rows, d), F32).at[:b].set(c).at[b].set(c_ctx)
    mods = _modulation(c_all, mod_w, mod_b).reshape(depth, mod_rows, 3, d)
    head_cols = -(-(d // HEAD) // 128) * 128
    down = (jnp.arange(d)[:, None] // HEAD == jnp.arange(head_cols)[None, :]).astype(BF16)
    sums = (down, down.T)

    col_major = False
    for i in range(depth):
        j = i // 2
        last = i == depth - 1
        col = j % 2 == 1
        want_col_major = col if i % 2 == 0 else not col
        if want_col_major != col_major:
            x = _to_col_major(x, rows) if want_col_major else _to_row_major(x, rows)
            col_major = want_col_major
        seg = rows if col else GRID_W
        mod_x = mods[i, :b]
        mod_c = jnp.broadcast_to(mods[i, b], (b, 3, d))
        gpre, gpost = g_pre[i][None], g_post[i][None]
        if i % 2 == 0:
            w_in = ev_w_in[j].astype(BF16)
            w_rkvz, w_sc = w_in[:, :4 * d], w_in[:, 4 * d:]
            w_out = ev_w_out[j].astype(BF16)
            cat = lambda w: jnp.concatenate([w[0], w[1]], axis=1).astype(BF16)
            stk = lambda w: jnp.concatenate([w[0], w[1]], axis=0).astype(BF16)
            k_a = ev_k_a[j][None]
            proj_args = (gpre, w_rkvz, cat(ev_w1[j]), stk(ev_w2[j]), cat(ev_a1[j]), stk(ev_a2[j]), ev_mu_rkv[j],
                         ev_mu_wa[j], ev_w0[j], ev_a0[j], ev_k_k[j][None], k_a, ev_r_k[j].reshape(2, d), *sums)
            pc = _rwkv_proj(ctx, mod_c, *proj_args, tt_ctx)
            px = _rwkv_proj(x, mod_x, *proj_args, TOKEN_TILE)
            zero = jnp.zeros((b, d // PAIR, PAIR, PAIR), F32)
            scan = lambda p, s0, rev, tt: _wkv(s0, p[0], p[1], p[2], p[3], p[6], p[7], k_a, rev, tt)
            oc_f, s_f = scan(pc, zero, False, min(SCAN_TILE, n_ctx))
            oc_b, s_b = scan(pc, zero, True, min(SCAN_TILE, n_ctx))
            ox_f, _ = scan(px, s_f, False, SCAN_TILE)
            ox_b, _ = scan(px, s_b, True, SCAN_TILE)
            yb_x = _sconv(x, mod_x, gpre, w_sc, ev_sc_w[j], seg, TOKEN_TILE)
            out_args = (ev_lnx_w[j][None], ev_lnx_b[j][None], gpost, w_out, *sums)
            x = _even_out(x, ox_f, ox_b, px[5], px[4], yb_x, mod_x, *out_args, TOKEN_TILE)
            if not last:
                yb_c = _sconv(ctx, mod_c, gpre, w_sc, ev_sc_w[j], n_ctx, tt_ctx)
                ctx = _even_out(ctx, oc_f, oc_b, pc[5], pc[4], yb_c, mod_c, *out_args, tt_ctx)
        else:
            odd_args = (gpre, od_w_in[j].astype(BF16), od_dw_w[j], od_dw_b[j][None], od_ln_w[j][None],
                        od_ln_b[j][None], od_w_out[j].astype(BF16), gpost)
            if not last:
                ctx_new = _conformer(ctx, mod_c, *odd_args, n_ctx, tt_ctx)
            x = _conformer_tm(x, mod_x, *odd_args, seg)
            if not last:
                ctx = ctx_new
    if col_major:
        x = _to_row_major(x, rows)
    return x
```

```python
import functools

import jax
import jax.numpy as jnp
from jax import lax
from jax.experimental import pallas as pl
from jax.experimental.pallas import tpu as pltpu

F32 = jnp.float32
BF16 = jnp.bfloat16

GRID_W = 64
HEAD = 64
PAIR = 2 * HEAD
CHUNK = 64
LORA = 64
RMS_EPS = 1e-6
LN_EPS = 1e-5
GN_EPS = 64e-5
HALO = 8
TOKEN_TILE = 256
SCAN_TILE = 256
SCAN_PAIRS = 4
CONV_PAD = 16
CONV_COLS = 512
VMEM_LIMIT = 56 * 1024 * 1024


def _dot(a, b):
    return jnp.dot(a.astype(BF16), b.astype(BF16), preferred_element_type=F32)


def _dot_nt(a, b):
    return lax.dot_general(a.astype(BF16), b.astype(BF16), (((1,), (1,)), ((), ())),
                           preferred_element_type=F32)


def _dot_tn(a, b):
    return lax.dot_general(a.astype(BF16), b.astype(BF16), (((0,), (0,)), ((), ())),
                           preferred_element_type=F32)


def _split3(x):
    hi = x.astype(BF16)
    r1 = x - hi.astype(F32)
    mid = r1.astype(BF16)
    lo = (r1 - mid.astype(F32)).astype(BF16)
    return hi, mid, lo


def _dot_x_exact(x, e):
    hi, mid, lo = _split3(x)
    f = lambda p: jnp.dot(p, e, preferred_element_type=F32)
    return f(hi) + f(mid) + f(lo)


def _head_sums(x, down, up):
    def two_pass(y, e):
        hi = y.astype(BF16)
        mid = (y - hi.astype(F32)).astype(BF16)
        return jnp.dot(hi, e, preferred_element_type=F32) + jnp.dot(mid, e, preferred_element_type=F32)
    return two_pass(two_pass(x, down), up)


def _dot_exact_x(e, x):
    hi, mid, lo = _split3(x)
    f = lambda p: jnp.dot(e, p, preferred_element_type=F32)
    return f(hi) + f(mid) + f(lo)


def _sigmoid(x):
    return 1.0 / (1.0 + jnp.exp(-x))


def _silu(x):
    return x * _sigmoid(x)


def _rms(x, g):
    return x * lax.rsqrt(jnp.mean(x * x, axis=-1, keepdims=True) + RMS_EPS) * g


def _modulated(x, mod_ref, g_ref):
    return _rms(x, g_ref[...]) * (1.0 + mod_ref[0, 1:2, :]) + mod_ref[0, 0:1, :]


def _const_spec(shape):
    return pl.BlockSpec(shape, lambda *_: (0,) * len(shape), pipeline_mode=pl.Buffered(1))


def _tok_spec(tt, d):
    return pl.BlockSpec((1, tt, d), lambda b, t: (b, t, 0))


def _mod_spec(d):
    return pl.BlockSpec((1, 3, d), lambda b, t: (b, 0, 0))


def _mod_kernel(c_ref, w_ref, b_ref, o_ref):
    o_ref[0] = _dot(_silu(c_ref[...]), w_ref[0]) + b_ref[0]


def _modulation(c_all, mod_w, mod_b):
    depth, d, d3 = mod_w.shape
    rows = c_all.shape[0]
    return pl.pallas_call(
        _mod_kernel,
        grid=(depth, d3 // d),
        in_specs=[pl.BlockSpec((rows, d), lambda i, n: (0, 0)),
                  pl.BlockSpec((1, d, d), lambda i, n: (i, 0, n)),
                  pl.BlockSpec((1, 1, d), lambda i, n: (i, 0, n))],
        out_specs=pl.BlockSpec((1, rows, d), lambda i, n: (i, 0, n)),
        out_shape=jax.ShapeDtypeStruct((depth, rows, d3), F32),
        name="modulation",
    )(c_all, mod_w, mod_b.reshape(depth, 1, d3))


def _rwkv_proj_kernel(x_ref, xp_ref, xn_ref, mod_ref, g_ref, w_ref, w1_ref, w2_ref, a1_ref, a2_ref,
                      mu_rkv_ref, mu_wa_ref, w0_ref, a0_ref, kk_ref, ka_ref, rk_ref, down_ref, up_ref,
                      r_out, k_out, v_out, kkn_out, sza_out, bonus_out, lw_out, a_out,
                      hext_ref, pext_ref):
    t = pl.program_id(1)
    nt = pl.num_programs(1)
    tt, d = x_ref.shape[1], x_ref.shape[2]
    h = _modulated(x_ref[0], mod_ref, g_ref)
    hext_ref[0:HALO] = jnp.where(t > 0, _modulated(xp_ref[0], mod_ref, g_ref), 0.0)
    hext_ref[HALO:HALO + tt] = h
    hext_ref[HALO + tt:] = jnp.where(t < nt - 1, _modulated(xn_ref[0], mod_ref, g_ref), 0.0)

    def shifted(ref):
        return 0.5 * (ref[pl.ds(HALO - 1, tt), :] + ref[pl.ds(HALO + 1, tt), :])

    dh = shifted(hext_ref) - h
    lane = lax.broadcasted_iota(jnp.int32, (tt, 2 * LORA), 1)

    def lora(xin, down_ref, up_ref, act):
        mid = act(_dot(xin, down_ref[...]))
        up = up_ref[...]
        return (_dot(jnp.where(lane < LORA, mid, 0.0), up), _dot(jnp.where(lane >= LORA, mid, 0.0), up))

    lw = lora(h + dh * mu_wa_ref[0:1, :], w1_ref, w2_ref, jnp.tanh)
    la = lora(h + dh * mu_wa_ref[1:2, :], a1_ref, a2_ref, lambda z: z)
    a_dir = []
    for z in range(2):
        lw_out[z, 0] = -jnp.exp(-0.5) * _sigmoid(w0_ref[z:z + 1, :] + lw[z])
        a_z = _sigmoid(a0_ref[z:z + 1, :] + la[z])
        a_out[z, 0] = a_z
        a_dir.append(a_z)

    hext_b = hext_ref[...].astype(BF16)
    hb = h.astype(BF16)
    mixed = []
    for g in range(3):
        pext_ref[...] = jnp.dot(hext_b, w_ref[:, g * d:(g + 1) * d], preferred_element_type=F32)
        centre = pext_ref[pl.ds(HALO, tt), :]
        mixed.append(centre + (shifted(pext_ref) - centre) * mu_rkv_ref[g:g + 1, :])
    r, k, v = mixed
    r_out[0] = r
    k_out[0] = k
    v_out[0] = v
    sza_out[0] = _silu(jnp.dot(hb, w_ref[:, 3 * d:4 * d], preferred_element_type=F32))

    down, up = down_ref[...], up_ref[...]
    kq = k * kk_ref[...]
    kkn_out[0] = kq * lax.rsqrt(jnp.maximum(_head_sums(kq * kq, down, up), 1e-24))
    ka = ka_ref[...]
    bsum = r * k * ((1.0 + (a_dir[0] - 1.0) * ka) * rk_ref[0:1, :] + (1.0 + (a_dir[1] - 1.0) * ka) * rk_ref[1:2, :])
    bonus_out[0] = _head_sums(bsum, down, up) * v


def _rwkv_proj(x, mod, g_pre, w_rkvz, w1c, w2c, a1c, a2c, mu_rkv, mu_wa, w0, a0, k_k, k_a, r_k, down, up, tt):
    b, t, d = x.shape
    nh = t // HALO
    step = tt // HALO
    vec = lambda n: _const_spec((n, d))
    tok = _tok_spec(tt, d)
    tok2 = pl.BlockSpec((2, 1, tt, d), lambda bi, ti: (0, bi, ti, 0))
    sd = jax.ShapeDtypeStruct((b, t, d), F32)
    sd2 = jax.ShapeDtypeStruct((2, b, t, d), F32)
    return pl.pallas_call(
        _rwkv_proj_kernel,
        grid=(b, t // tt),
        in_specs=[tok,
                  pl.BlockSpec((1, HALO, d), lambda bi, ti: (bi, jnp.maximum(ti * step - 1, 0), 0)),
                  pl.BlockSpec((1, HALO, d), lambda bi, ti: (bi, jnp.minimum((ti + 1) * step, nh - 1), 0)),
                  _mod_spec(d), vec(1),
                  _const_spec((d, 4 * d)), _const_spec((d, 2 * LORA)), _const_spec((2 * LORA, d)),
                  _const_spec((d, 2 * LORA)), _const_spec((2 * LORA, d)),
                  vec(3), vec(2), vec(2), vec(2), vec(1), vec(1), vec(2),
                  _const_spec(down.shape), _const_spec(up.shape)],
        out_specs=[tok, tok, tok, tok, tok, tok, tok2, tok2],
        out_shape=[sd, sd, sd, sd, sd, sd, sd2, sd2],
        scratch_shapes=[pltpu.VMEM((tt + 2 * HALO, d), F32), pltpu.VMEM((tt + 2 * HALO, d), F32)],
        compiler_params=pltpu.CompilerParams(dimension_semantics=("parallel", "parallel"),
                                             vmem_limit_bytes=VMEM_LIMIT),
        name="rwkv_proj",
    )(x, x, x, mod, g_pre, w_rkvz, w1c, w2c, a1c, a2c, mu_rkv, mu_wa, w0, a0, k_k, k_a, r_k, down, up)


def _sconv_kernel(x_ref, mod_ref, g_ref, w_ref, scw_ref, o_ref, uext_ref, *, seg):
    tt, d = x_ref.shape[1], x_ref.shape[2]
    hb = _modulated(x_ref[0], mod_ref, g_ref).astype(BF16)
    proj = lambda i: jnp.dot(hb, w_ref[:, i * d:(i + 1) * d], preferred_element_type=F32)
    u = proj(1) * proj(2)
    zero = jnp.zeros((HALO, d), F32)
    uext_ref[0:HALO] = zero
    uext_ref[HALO:HALO + tt] = u
    uext_ref[HALO + tt:] = zero
    pos = lax.broadcasted_iota(jnp.int32, (tt, d), 0) & (seg - 1)
    prev = jnp.where(pos > 0, uext_ref[pl.ds(HALO - 1, tt), :], 0.0)
    nxt = jnp.where(pos < seg - 1, uext_ref[pl.ds(HALO + 1, tt), :], 0.0)
    conv = scw_ref[0:1, :] * prev + scw_ref[1:2, :] * u + scw_ref[2:3, :] * nxt
    o_ref[0] = proj(0) * conv * _silu(proj(3))


def _sconv(x, mod, g_pre, w_sc, sc_w, seg, tt):
    b, t, d = x.shape
    return pl.pallas_call(
        functools.partial(_sconv_kernel, seg=seg),
        grid=(b, t // tt),
        in_specs=[_tok_spec(tt, d), _mod_spec(d), _const_spec((1, d)), _const_spec((d, 4 * d)),
                  _const_spec((3, d))],
        out_specs=_tok_spec(tt, d),
        out_shape=jax.ShapeDtypeStruct((b, t, d), F32),
        scratch_shapes=[pltpu.VMEM((tt + 2 * HALO, d), F32)],
        compiler_params=pltpu.CompilerParams(dimension_semantics=("parallel", "parallel"),
                                             vmem_limit_bytes=VMEM_LIMIT),
        name="sconv",
    )(x, mod, g_pre, w_sc, sc_w)


TICK_EVERY = 4


def _each(f, *lists, tick=None):
    out = []
    for n, xs in enumerate(zip(*lists)):
        out.append(f(*xs))
        if tick is not None and n % TICK_EVERY == TICK_EVERY - 1:
            tick()
    return out


def _stack_heads(z):
    first = (lax.broadcasted_iota(jnp.int32, z.shape, 1) & HEAD) == 0
    return jnp.concatenate([jnp.where(first, z, 0.0), jnp.where(first, 0.0, z)], axis=0)


def _pair_dot(x, y):
    return _dot(x, _stack_heads(y))


def _unit_triangular_solves(mats, rhs, tick):
    rows = lax.broadcasted_iota(jnp.int32, mats[0].shape, 0)
    cols = lax.broadcasted_iota(jnp.int32, mats[0].shape, 1) & (HEAD - 1)
    eye = (rows == cols).astype(F32)
    blk16 = (rows >> 4) == (cols >> 4)
    n = mats[0].shape[1]
    dg = [jnp.where(blk16, a, 0.0) for a in mats]
    off = _each(lambda a, d: a - d, mats, dg)
    wide = lambda x, y: _pair_dot(x, jnp.concatenate([x, y], axis=1))
    d2 = _each(_pair_dot, dg, dg, tick=tick)
    tm = [eye + d for d in dg]
    both = _each(wide, d2, tm, tick=tick)
    d4, tm = [x[:, 0:n] for x in both], _each(lambda t0, x: t0 + x[:, n:], tm, both)
    both = _each(wide, d4, tm, tick=tick)
    d8, tm = [x[:, 0:n] for x in both], _each(lambda t0, x: t0 + x[:, n:], tm, both)
    tm = _each(lambda p, t0: t0 + _pair_dot(p, t0), d8, tm, tick=tick)
    nb = _each(_pair_dot, tm, off, tick=tick)
    x = _each(_pair_dot, tm, rhs, tick=tick)
    nb2 = _each(_pair_dot, nb, nb, tick=tick)
    x = _each(lambda p, x0: x0 + _pair_dot(p, x0), nb, x, tick=tick)
    return _each(lambda p, x0: x0 + _pair_dot(p, x0), nb2, x, tick=tick)


def _wkv_kernel(s0_ref, r_ref, k_ref, v_ref, kk_ref, lw_ref, a_ref, ka_ref, o_ref, s_out, s_ref, *stage_refs,
                nt, reverse):
    t = pl.program_id(2)
    tt, width = r_ref.shape[1], r_ref.shape[2]
    nc, npair = tt // CHUNK, width // PAIR
    order = list(range(nc - 1, -1, -1) if reverse else range(nc))
    units = [(p, j) for j in order for p in range(npair)]
    half = len(stage_refs) // 2
    stage = (stage_refs[:half], stage_refs[half:])

    @pl.when(t == 0)
    def _():
        s_ref[...] = s0_ref[0]
        for ref in stage[1][:-1]:
            ref[...] = jnp.zeros(ref.shape, ref.dtype)
        stage[1][-1][...] = jnp.ones(stage[1][-1].shape, F32)

    def recurrence(refs):
        reader_ref, w0_ref, vs_ref, v_ref2, arbk_ref, bkg_ref, gend_ref = refs
        same_head = ((lax.broadcasted_iota(jnp.int32, (PAIR, PAIR), 0) >> 6)
                     == (lax.broadcasted_iota(jnp.int32, (PAIR, PAIR), 1) >> 6))
        s = [s_ref[p] for p in range(npair)]
        for i, (p, j) in enumerate(units):
            sh = _dot_nt(reader_ref[i], s[p])
            yield
            u = sh[0:CHUNK] + w0_ref[i]
            uv_stacked = jnp.concatenate([_stack_heads(u).astype(BF16), vs_ref[i]], axis=0)
            o_ref[0, j * CHUNK:(j + 1) * CHUNK, p * PAIR:(p + 1) * PAIR] = (
                sh[CHUNK:] + jnp.dot(arbk_ref[i], uv_stacked, preferred_element_type=F32))
            upd = _dot_tn(jnp.concatenate([u.astype(BF16), v_ref2[i]], axis=0), bkg_ref[i])
            s[p] = s[p] * gend_ref[i] + jnp.where(same_head, upd, 0.0)
            yield
        for p in range(npair):
            s_ref[p] = s[p]

    @pl.when(t == nt)
    def _():
        for _ in recurrence(stage[(nt - 1) % 2]):
            pass
        for p in range(npair):
            s_out[0, p] = s_ref[p]

    for parity in range(2):
        @pl.when((t < nt) & (t % 2 == parity))
        def _():
            steps = recurrence(stage[1 - parity])
            _wkv_prepare(r_ref, k_ref, v_ref, kk_ref, lw_ref, a_ref, ka_ref, stage[parity], units, reverse,
                         lambda: next(steps, None))
            for _ in steps:
                pass


def _wkv_prepare(r_ref, k_ref, v_ref, kk_ref, lw_ref, a_ref, ka_ref, stage, units, reverse, tick):
    reader_ref, w0_ref, vs_ref, v_ref2, arbk_ref, bkg_ref, gend_ref = stage
    tt = r_ref.shape[1]
    ri = lax.broadcasted_iota(jnp.int32, (CHUNK, PAIR), 0)
    ci = lax.broadcasted_iota(jnp.int32, (CHUNK, PAIR), 1) & (HEAD - 1)
    rt, ct = lax.broadcasted_iota(jnp.int32, (tt, tt), 0), lax.broadcasted_iota(jnp.int32, (tt, tt), 1)
    same_chunk = (rt >> 6) == (ct >> 6)
    if reverse:
        strict, incl, tri = ri < ci, ri <= ci, (same_chunk & (rt <= ct)).astype(BF16)
    else:
        strict, incl, tri = ri > ci, ri >= ci, (same_chunk & (rt >= ct)).astype(BF16)

    lw, a, kk = lw_ref[0, 0], a_ref[0, 0], kk_ref[0]
    cum = _dot_exact_x(tri, lw)
    g_inv = jnp.exp(-cum)
    a_t = -kk * jnp.exp(cum - lw)
    r_t = r_ref[0] * jnp.exp(cum)
    b_t = kk * a * g_inv
    k_t = k_ref[0] * (1.0 + (a - 1.0) * ka_ref[...]) * g_inv
    v_t = v_ref[0]

    cut = lambda z: [z[j * CHUNK:(j + 1) * CHUNK, p * PAIR:(p + 1) * PAIR] for p, j in units]
    a1, r1, b1, k1, v1 = cut(a_t), cut(r_t), cut(b_t), cut(k_t), cut(v_t)
    end_row = lambda j: j * CHUNK if reverse else (j + 1) * CHUNK - 1
    g_end = [jnp.exp(cum[end_row(j):end_row(j) + 1, p * PAIR:(p + 1) * PAIR]) for p, j in units]
    ar = _each(lambda x, y: jnp.concatenate([x, y], axis=0), a1, r1)
    bk = _each(lambda x, y: jnp.concatenate([x, y], axis=0), b1, k1)
    sc = _each(lambda x, b, k: _dot_nt(x, jnp.concatenate([_stack_heads(b), _stack_heads(k)], axis=0)),
               ar, b1, k1, tick=tick)
    a_ab = [jnp.where(strict, x[0:CHUNK, 0:PAIR], 0.0) for x in sc]
    a_ak = [jnp.where(strict, x[0:CHUNK, PAIR:], 0.0) for x in sc]
    a_rbk = [jnp.concatenate([jnp.where(incl, x[CHUNK:, 0:PAIR], 0.0), jnp.where(incl, x[CHUNK:, PAIR:], 0.0)],
                             axis=1).astype(BF16) for x in sc]
    w_ak = _each(_pair_dot, a_ak, v1, tick=tick)
    rhs = _each(lambda x, w: jnp.concatenate([x, w], axis=1), a1, w_ak)
    pre = _unit_triangular_solves(a_ab, rhs, tick)
    for i in range(len(units)):
        reader_ref[i] = jnp.concatenate([pre[i][:, 0:PAIR], r1[i]], axis=0).astype(BF16)
        w0_ref[i] = pre[i][:, PAIR:]
        vs_ref[i] = _stack_heads(v1[i]).astype(BF16)
        v_ref2[i] = v1[i].astype(BF16)
        arbk_ref[i] = a_rbk[i]
        bkg_ref[i] = (bk[i] * g_end[i]).astype(BF16)
        gend_ref[i] = g_end[i]


def _wkv(s0, r, k, v, kk, lw, a, k_a, reverse, tt):
    b, t, d = r.shape
    nt = t // tt
    z = 1 if reverse else 0
    npair = min(SCAN_PAIRS, d // PAIR)
    width = npair * PAIR
    n_units = npair * (tt // CHUNK)
    tpos = (lambda ti: nt - 1 - ti) if reverse else (lambda ti: ti)
    t_in = lambda ti: tpos(jnp.minimum(ti, nt - 1))
    t_out = lambda ti: tpos(jnp.maximum(ti - 1, 0))
    tok = pl.BlockSpec((1, tt, width), lambda bi, p, ti: (bi, t_in(ti), p))
    tokz = pl.BlockSpec((1, 1, tt, width), lambda bi, p, ti: (z, bi, t_in(ti), p))
    tok_out = pl.BlockSpec((1, tt, width), lambda bi, p, ti: (bi, t_out(ti), p))
    st = pl.BlockSpec((1, npair, PAIR, PAIR), lambda bi, p, ti: (bi, p, 0, 0))
    return pl.pallas_call(
        functools.partial(_wkv_kernel, nt=nt, reverse=reverse),
        grid=(b, d // width, nt + 1),
        in_specs=[st, tok, tok, tok, tok, tokz, tokz, pl.BlockSpec((1, width), lambda bi, p, ti: (0, p))],
        out_specs=[tok_out, st],
        out_shape=[jax.ShapeDtypeStruct((b, t, d), F32), jax.ShapeDtypeStruct((b, d // PAIR, PAIR, PAIR), F32)],
        scratch_shapes=[pltpu.VMEM((npair, PAIR, PAIR), F32)] + 2 * [
            pltpu.VMEM((n_units, 2 * CHUNK, PAIR), BF16), pltpu.VMEM((n_units, CHUNK, PAIR), F32),
            pltpu.VMEM((n_units, 2 * CHUNK, PAIR), BF16), pltpu.VMEM((n_units, CHUNK, PAIR), BF16),
            pltpu.VMEM((n_units, CHUNK, 2 * PAIR), BF16), pltpu.VMEM((n_units, 2 * CHUNK, PAIR), BF16),
            pltpu.VMEM((n_units, 1, PAIR), F32)],
        compiler_params=pltpu.CompilerParams(dimension_semantics=("parallel", "parallel", "arbitrary"),
                                             vmem_limit_bytes=VMEM_LIMIT),
        name="wkv_bwd" if reverse else "wkv_fwd",
    )(s0, r, k, v, kk, lw, a, k_a)


def _even_out_kernel(x_ref, of_ref, ob_ref, bonus_ref, sza_ref, yb_ref, mod_ref, lnw_ref, lnb_ref, gp_ref,
                     w_ref, down_ref, up_ref, o_ref):
    d = x_ref.shape[2]
    down, up = down_ref[...], up_ref[...]
    o = of_ref[0] + ob_ref[0]
    dev = o - _head_sums(o, down, up) * (1.0 / HEAD)
    var = _head_sums(dev * dev, down, up) * (1.0 / HEAD)
    ya = (dev * lax.rsqrt(var + GN_EPS) * lnw_ref[...] + lnb_ref[...] + bonus_ref[0]) * sza_ref[0]
    y = _dot(ya, w_ref[0:d, :]) + _dot(yb_ref[0], w_ref[d:2 * d, :])
    o_ref[0] = x_ref[0] + mod_ref[0, 2:3, :] * _rms(y, gp_ref[...])


def _even_out(x, o_f, o_b, bonus, sza, ybg, mod, lnx_w, lnx_b, g_post, w_out, down, up, tt):
    b, t, d = x.shape
    tok = _tok_spec(tt, d)
    vec = _const_spec((1, d))
    return pl.pallas_call(
        _even_out_kernel,
        grid=(b, t // tt),
        in_specs=[tok, tok, tok, tok, tok, tok, _mod_spec(d), vec, vec, vec,
                  _const_spec((2 * d, d)), _const_spec(down.shape), _const_spec(up.shape)],
        out_specs=tok,
        out_shape=jax.ShapeDtypeStruct((b, t, d), F32),
        compiler_params=pltpu.CompilerParams(dimension_semantics=("parallel", "parallel"),
                                             vmem_limit_bytes=VMEM_LIMIT),
        name="even_out",
    )(x, o_f, o_b, bonus, sza, ybg, mod, lnx_w, lnx_b, g_post, w_out, down, up)


def _conformer_kernel(x_ref, mod_ref, g_ref, w_in_ref, dww_ref, dwb_ref, lnw_ref, lnb_ref, w_out_ref, gp_ref,
                      o_ref, pad_ref, conv_ref, shift_ref, *, seg):
    tt, d = x_ref.shape[1], x_ref.shape[2]
    e = dww_ref.shape[1]
    taps = dww_ref.shape[0]
    span = seg + 2 * CONV_PAD
    x = x_ref[0]
    hb = _modulated(x, mod_ref, g_ref).astype(BF16)
    proj = lambda i: jnp.dot(hb, w_in_ref[:, i * e:(i + 1) * e], preferred_element_type=F32)
    glu = proj(0) * _sigmoid(proj(1))
    zero = jnp.zeros((CONV_PAD, e), F32)
    for s in range(tt // seg):
        pad_ref[s * span:s * span + CONV_PAD] = zero
        pad_ref[s * span + CONV_PAD:s * span + CONV_PAD + seg] = glu[s * seg:(s + 1) * seg]
        pad_ref[s * span + CONV_PAD + seg:(s + 1) * span] = zero

    rows = min(seg, CHUNK)
    lead = CONV_PAD - taps // 2
    win = shift_ref.shape[1]
    for s in range(tt // seg):
        for r0 in range(0, seg, rows):
            for c0 in range(0, e, CONV_COLS):
                cols = pl.ds(c0, CONV_COLS)
                for phase in range(8):
                    shift_ref[phase] = pad_ref[pl.ds(s * span + r0 + phase, win), cols]
                acc = jnp.zeros((rows, CONV_COLS), F32)
                for j in range(taps):
                    q, phase = divmod(lead + j, 8)
                    acc = acc + dww_ref[j:j + 1, cols] * shift_ref[phase, pl.ds(8 * q, rows), :]
                conv_ref[pl.ds(s * seg + r0, rows), cols] = acc

    conv = conv_ref[...] + dwb_ref[...]
    dev = conv - jnp.mean(conv, axis=-1, keepdims=True)
    var = jnp.mean(dev * dev, axis=-1, keepdims=True)
    act = _silu(dev * lax.rsqrt(var + LN_EPS) * lnw_ref[...] + lnb_ref[...]) * _silu(proj(2))
    o_ref[0] = x + mod_ref[0, 2:3, :] * _rms(_dot(act, w_out_ref[...]), gp_ref[...])


def _conformer(x, mod, g_pre, w_in, dw_w, dw_b, ln_w, ln_b, w_out, g_post, seg, tt):
    b, t, d = x.shape
    taps, e = dw_w.shape
    assert taps // 2 <= CONV_PAD and tt % seg == 0
    rows = min(seg, CHUNK)
    win = rows + 8 * ((CONV_PAD + taps // 2) // 8)
    assert 7 + win <= rows + 2 * CONV_PAD
    tok = _tok_spec(tt, d)
    return pl.pallas_call(
        functools.partial(_conformer_kernel, seg=seg),
        grid=(b, t // tt),
        in_specs=[tok, _mod_spec(d), _const_spec((1, d)), _const_spec((d, 3 * e)), _const_spec((taps, e)),
                  _const_spec((1, e)), _const_spec((1, e)), _const_spec((1, e)), _const_spec((e, d)),
                  _const_spec((1, d))],
        out_specs=tok,
        out_shape=jax.ShapeDtypeStruct((b, t, d), F32),
        scratch_shapes=[pltpu.VMEM(((tt // seg) * (seg + 2 * CONV_PAD), e), F32), pltpu.VMEM((tt, e), F32),
                        pltpu.VMEM((8, win, CONV_COLS), F32)],
        compiler_params=pltpu.CompilerParams(dimension_semantics=("parallel", "parallel"),
                                             vmem_limit_bytes=VMEM_LIMIT),
        name="conformer",
    )(x, mod, g_pre, w_in, dw_w, dw_b, ln_w, ln_b, w_out, g_post)


def _conformer_tm_kernel(x_ref, mod_ref, g_ref, w_in_ref, dww_ref, dwb_ref, lnw_ref, lnb_ref, w_out_ref, gp_ref,
                         o_ref, pad_ref, conv_ref, gate_ref):
    seg, nseg, d = x_ref.shape
    n = seg * nseg
    taps, e = dww_ref.shape[0], dww_ref.shape[2]
    x = x_ref[...].reshape(n, d)
    hb = _modulated(x, mod_ref, g_ref).astype(BF16)
    proj = lambda lo, hi: jnp.dot(hb, w_in_ref[:, lo:hi], preferred_element_type=F32)
    zero = jnp.zeros((CONV_PAD, nseg, e), F32)
    pad_ref[0:CONV_PAD] = zero
    pad_ref[CONV_PAD + seg:] = zero
    for c0 in range(0, e, CONV_COLS):
        glu = proj(c0, c0 + CONV_COLS) * _sigmoid(proj(e + c0, e + c0 + CONV_COLS))
        pad_ref[CONV_PAD:CONV_PAD + seg, :, c0:c0 + CONV_COLS] = glu.reshape(seg, nseg, CONV_COLS)

    lead = CONV_PAD - taps // 2

    for c0 in range(0, e, CONV_COLS):
        cols = pl.ds(c0, CONV_COLS)
        for t0 in range(0, seg, 8):
            acc = jnp.zeros((8, nseg, CONV_COLS), F32)
            for j in range(taps):
                acc = acc + dww_ref[j, :, cols][None] * pad_ref[pl.ds(t0 + lead + j, 8), :, cols]
            conv_ref[pl.ds(t0, 8), :, cols] = acc
        gate_ref[:, c0:c0 + CONV_COLS] = _silu(proj(2 * e + c0, 2 * e + c0 + CONV_COLS))

    conv = conv_ref[...].reshape(n, e) + dwb_ref[...]
    dev = conv - jnp.mean(conv, axis=-1, keepdims=True)
    var = jnp.mean(dev * dev, axis=-1, keepdims=True)
    act = _silu(dev * lax.rsqrt(var + LN_EPS) * lnw_ref[...] + lnb_ref[...]) * gate_ref[...]
    out = x + mod_ref[0, 2:3, :] * _rms(_dot(act, w_out_ref[...]), gp_ref[...])
    o_ref[...] = out.reshape(seg, nseg, d)


def _conformer_tm(x, mod, g_pre, w_in, dw_w, dw_b, ln_w, ln_b, w_out, g_post, seg):
    b, t, d = x.shape
    taps, e = dw_w.shape
    nseg = 8
    groups = t // (seg * nseg)
    assert taps // 2 <= CONV_PAD and seg % 8 == 0 and groups * seg * nseg == t
    x5 = x.reshape(b, seg, groups, nseg, d)
    tile = pl.BlockSpec((None, seg, None, nseg, d), lambda bi, gi: (bi, 0, gi, 0, 0))
    out = pl.pallas_call(
        _conformer_tm_kernel,
        grid=(b, groups),
        in_specs=[tile, _mod_spec(d), _const_spec((1, d)), _const_spec((d, 3 * e)), _const_spec((taps, nseg, e)),
                  _const_spec((1, e)), _const_spec((1, e)), _const_spec((1, e)), _const_spec((e, d)),
                  _const_spec((1, d))],
        out_specs=tile,
        out_shape=jax.ShapeDtypeStruct(x5.shape, F32),
        scratch_shapes=[pltpu.VMEM((seg + 2 * CONV_PAD, nseg, e), F32), pltpu.VMEM((seg, nseg, e), F32),
                        pltpu.VMEM((seg * nseg, e), F32)],
        compiler_params=pltpu.CompilerParams(dimension_semantics=("parallel", "parallel"),
                                             vmem_limit_bytes=VMEM_LIMIT),
        name="conformer_tm",
    )(x5, mod, g_pre, w_in, jnp.broadcast_to(dw_w[:, None, :], (taps, nseg, e)), dw_b, ln_w, ln_b, w_out, g_post)
    return out.reshape(b, t, d)


def _to_col_major(h, rows):
    b, t, ch = h.shape
    return h.reshape(b, rows, GRID_W, ch).swapaxes(1, 2).reshape(b, t, ch)


def _to_row_major(h, rows):
    b, t, ch = h.shape
    return h.reshape(b, GRID_W, rows, ch).swapaxes(1, 2).reshape(b, t, ch)


def kernel(x, c, ctx, c_ctx, mod_w, mod_b, g_pre, g_post, ev_w_in, ev_w_out, ev_mu_rkv, ev_mu_wa, ev_w0, ev_w1, ev_w2, ev_a0, ev_a1, ev_a2, ev_k_k, ev_k_a, ev_r_k, ev_lnx_w, ev_lnx_b, ev_sc_w, od_w_in, od_dw_w, od_dw_b, od_ln_w, od_ln_b, od_w_out):
    b, t, d = x.shape
    n_ctx = ctx.shape[1]
    depth = mod_w.shape[0]
    rows = t // GRID_W
    assert d % PAIR == 0 and t % TOKEN_TILE == 0 and n_ctx % CHUNK == 0
    tt_ctx = min(TOKEN_TILE, n_ctx)

    mod_rows = -(-(b + 1) // 8) * 8
    c_all = jnp.zeros((mod_rows, d), F32).at[:b].set(c).at[b].set(c_ctx)
    mods = _modulation(c_all, mod_w, mod_b).reshape(depth, mod_rows, 3, d)
    head_cols = -(-(d // HEAD) // 128) * 128
    down = (jnp.arange(d)[:, None] // HEAD == jnp.arange(head_cols)[None, :]).astype(BF16)
    sums = (down, down.T)

    col_major = False
    for i in range(depth):
        j = i // 2
        last = i == depth - 1
        col = j % 2 == 1
        want_col_major = col if i % 2 == 0 else not col
        if want_col_major != col_major:
            x = _to_col_major(x, rows) if want_col_major else _to_row_major(x, rows)
            col_major = want_col_major
        seg = rows if col else GRID_W
        mod_x = mods[i, :b]
        mod_c = jnp.broadcast_to(mods[i, b], (b, 3, d))
        gpre, gpost = g_pre[i][None], g_post[i][None]
        if i % 2 == 0:
            w_in = ev_w_in[j].astype(BF16)
            w_rkvz, w_sc = w_in[:, :4 * d], w_in[:, 4 * d:]
            w_out = ev_w_out[j].astype(BF16)
            cat = lambda w: jnp.concatenate([w[0], w[1]], axis=1).astype(BF16)
            stk = lambda w: jnp.concatenate([w[0], w[1]], axis=0).astype(BF16)
            k_a = ev_k_a[j][None]
            proj_args = (gpre, w_rkvz, cat(ev_w1[j]), stk(ev_w2[j]), cat(ev_a1[j]), stk(ev_a2[j]), ev_mu_rkv[j],
                         ev_mu_wa[j], ev_w0[j], ev_a0[j], ev_k_k[j][None], k_a, ev_r_k[j].reshape(2, d), *sums)
            pc = _rwkv_proj(ctx, mod_c, *proj_args, tt_ctx)
            px = _rwkv_proj(x, mod_x, *proj_args, TOKEN_TILE)
            zero = jnp.zeros((b, d // PAIR, PAIR, PAIR), F32)
            scan = lambda p, s0, rev, tt: _wkv(s0, p[0], p[1], p[2], p[3], p[6], p[7], k_a, rev, tt)
            oc_f, s_f = scan(pc, zero, False, min(SCAN_TILE, n_ctx))
            oc_b, s_b = scan(pc, zero, True, min(SCAN_TILE, n_ctx))
            ox_f, _ = scan(px, s_f, False, SCAN_TILE)
            ox_b, _ = scan(px, s_b, True, SCAN_TILE)
            yb_x = _sconv(x, mod_x, gpre, w_sc, ev_sc_w[j], seg, TOKEN_TILE)
            out_args = (ev_lnx_w[j][None], ev_lnx_b[j][None], gpost, w_out, *sums)
            x = _even_out(x, ox_f, ox_b, px[5], px[4], yb_x, mod_x, *out_args, TOKEN_TILE)
            if not last:
                yb_c = _sconv(ctx, mod_c, gpre, w_sc, ev_sc_w[j], n_ctx, tt_ctx)
                ctx = _even_out(ctx, oc_f, oc_b, pc[5], pc[4], yb_c, mod_c, *out_args, tt_ctx)
        else:
            odd_args = (gpre, od_w_in[j].astype(BF16), od_dw_w[j], od_dw_b[j][None], od_ln_w[j][None],
                        od_ln_b[j][None], od_w_out[j].astype(BF16), gpost)
            if not last:
                ctx_new = _conformer(ctx, mod_c, *odd_args, n_ctx, tt_ctx)
            x = _conformer_tm(x, mod_x, *odd_args, seg)
            if not last:
                ctx = ctx_new
    if col_major:
        x = _to_row_major(x, rows)
    return x
```

```python
import functools

import jax
import jax.numpy as jnp
from jax import lax
from jax.experimental import pallas as pl
from jax.experimental.pallas import tpu as pltpu

F32 = jnp.float32
BF16 = jnp.bfloat16

GRID_W = 64
HEAD = 64
PAIR = 2 * HEAD
CHUNK = 64
LORA = 64
RMS_EPS = 1e-6
LN_EPS = 1e-5
GN_EPS = 64e-5
HALO = 8
TOKEN_TILE = 256
SCAN_TILE = 256
SCAN_PAIRS = 4
CONV_PAD = 16
CONV_COLS = 512
VMEM_LIMIT = 56 * 1024 * 1024


def _dot(a, b):
    return jnp.dot(a.astype(BF16), b.astype(BF16), preferred_element_type=F32)


def _dot_nt(a, b):
    return lax.dot_general(a.astype(BF16), b.astype(BF16), (((1,), (1,)), ((), ())),
                           preferred_element_type=F32)


def _dot_tn(a, b):
    return lax.dot_general(a.astype(BF16), b.astype(BF16), (((0,), (0,)), ((), ())),
                           preferred_element_type=F32)


def _split3(x):
    hi = x.astype(BF16)
    r1 = x - hi.astype(F32)
    mid = r1.astype(BF16)
    lo = (r1 - mid.astype(F32)).astype(BF16)
    return hi, mid, lo


def _dot_x_exact(x, e):
    hi, mid, lo = _split3(x)
    f = lambda p: jnp.dot(p, e, preferred_element_type=F32)
    return f(hi) + f(mid) + f(lo)


def _head_sums(x, down, up):
    def two_pass(y, e):
        hi = y.astype(BF16)
        mid = (y - hi.astype(F32)).astype(BF16)
        return jnp.dot(hi, e, preferred_element_type=F32) + jnp.dot(mid, e, preferred_element_type=F32)
    return two_pass(two_pass(x, down), up)


def _dot_exact_x(e, x):
    hi, mid, lo = _split3(x)
    f = lambda p: jnp.dot(e, p, preferred_element_type=F32)
    return f(hi) + f(mid) + f(lo)


def _sigmoid(x):
    return 1.0 / (1.0 + jnp.exp(-x))


def _silu(x):
    return x * _sigmoid(x)


def _rms(x, g):
    return x * lax.rsqrt(jnp.mean(x * x, axis=-1, keepdims=True) + RMS_EPS) * g


def _modulated(x, mod_ref, g_ref):
    return _rms(x, g_ref[...]) * (1.0 + mod_ref[0, 1:2, :]) + mod_ref[0, 0:1, :]


def _params(*semantics):
    return pltpu.CompilerParams(dimension_semantics=semantics, vmem_limit_bytes=VMEM_LIMIT)


def _const_spec(shape):
    return pl.BlockSpec(shape, lambda *_: (0,) * len(shape), pipeline_mode=pl.Buffered(1))


def _tok_spec(tt, d):
    return pl.BlockSpec((1, tt, d), lambda b, t: (b, t, 0))


def _mod_spec(d):
    return pl.BlockSpec((1, 3, d), lambda b, t: (b, 0, 0))


def _mod_kernel(c_ref, w_ref, b_ref, o_ref):
    o_ref[0] = _dot(_silu(c_ref[...]), w_ref[0]) + b_ref[0]


def _modulation(c_all, mod_w, mod_b):
    depth, d, d3 = mod_w.shape
    rows = c_all.shape[0]
    return pl.pallas_call(
        _mod_kernel,
        grid=(depth, d3 // d),
        in_specs=[pl.BlockSpec((rows, d), lambda i, n: (0, 0)),
                  pl.BlockSpec((1, d, d), lambda i, n: (i, 0, n)),
                  pl.BlockSpec((1, 1, d), lambda i, n: (i, 0, n))],
        out_specs=pl.BlockSpec((1, rows, d), lambda i, n: (i, 0, n)),
        out_shape=jax.ShapeDtypeStruct((depth, rows, d3), F32),
        name="modulation",
    )(c_all, mod_w, mod_b.reshape(depth, 1, d3))


def _rwkv_proj_kernel(x_ref, xp_ref, xn_ref, mod_ref, g_ref, w_ref, w1_ref, w2_ref, a1_ref, a2_ref,
                      mu_rkv_ref, mu_wa_ref, w0_ref, a0_ref, kk_ref, ka_ref, rk_ref, down_ref, up_ref,
                      r_out, k_out, v_out, kkn_out, sza_out, bonus_out, lw_out, a_out,
                      hext_ref, pext_ref):
    t = pl.program_id(1)
    nt = pl.num_programs(1)
    tt, d = x_ref.shape[1], x_ref.shape[2]
    h = _modulated(x_ref[0], mod_ref, g_ref)
    hext_ref[0:HALO] = jnp.where(t > 0, _modulated(xp_ref[0], mod_ref, g_ref), 0.0)
    hext_ref[HALO:HALO + tt] = h
    hext_ref[HALO + tt:] = jnp.where(t < nt - 1, _modulated(xn_ref[0], mod_ref, g_ref), 0.0)

    def shifted(ref):
        return 0.5 * (ref[pl.ds(HALO - 1, tt), :] + ref[pl.ds(HALO + 1, tt), :])

    dh = shifted(hext_ref) - h
    lane = lax.broadcasted_iota(jnp.int32, (tt, 2 * LORA), 1)

    def lora(xin, down_ref, up_ref, act):
        mid = act(_dot(xin, down_ref[...]))
        up = up_ref[...]
        return (_dot(jnp.where(lane < LORA, mid, 0.0), up), _dot(jnp.where(lane >= LORA, mid, 0.0), up))

    lw = lora(h + dh * mu_wa_ref[0:1, :], w1_ref, w2_ref, jnp.tanh)
    la = lora(h + dh * mu_wa_ref[1:2, :], a1_ref, a2_ref, lambda z: z)
    a_dir = []
    for z in range(2):
        lw_out[z, 0] = -jnp.exp(-0.5) * _sigmoid(w0_ref[z:z + 1, :] + lw[z])
        a_z = _sigmoid(a0_ref[z:z + 1, :] + la[z])
        a_out[z, 0] = a_z
        a_dir.append(a_z)

    hext_b = hext_ref[...].astype(BF16)
    hb = h.astype(BF16)
    mixed = []
    for g in range(3):
        pext_ref[...] = jnp.dot(hext_b, w_ref[:, g * d:(g + 1) * d], preferred_element_type=F32)
        centre = pext_ref[pl.ds(HALO, tt), :]
        mixed.append(centre + (shifted(pext_ref) - centre) * mu_rkv_ref[g:g + 1, :])
    r, k, v = mixed
    r_out[0] = r.astype(BF16)
    k_out[0] = k.astype(BF16)
    v_out[0] = v.astype(BF16)
    sza_out[0] = _silu(jnp.dot(hb, w_ref[:, 3 * d:4 * d], preferred_element_type=F32))

    down, up = down_ref[...], up_ref[...]
    kq = k * kk_ref[...]
    kkn_out[0] = (kq * lax.rsqrt(jnp.maximum(_head_sums(kq * kq, down, up), 1e-24))).astype(BF16)
    ka = ka_ref[...]
    bsum = r * k * ((1.0 + (a_dir[0] - 1.0) * ka) * rk_ref[0:1, :] + (1.0 + (a_dir[1] - 1.0) * ka) * rk_ref[1:2, :])
    bonus_out[0] = _head_sums(bsum, down, up) * v


def _rwkv_proj(x, mod, g_pre, w_rkvz, w1c, w2c, a1c, a2c, mu_rkv, mu_wa, w0, a0, k_k, k_a, r_k, down, up, tt):
    b, t, d = x.shape
    nh = t // HALO
    step = tt // HALO
    vec = lambda n: _const_spec((n, d))
    tok = _tok_spec(tt, d)
    tok2 = pl.BlockSpec((2, 1, tt, d), lambda bi, ti: (0, bi, ti, 0))
    sd = jax.ShapeDtypeStruct((b, t, d), F32)
    sdb = jax.ShapeDtypeStruct((b, t, d), BF16)
    sd2 = jax.ShapeDtypeStruct((2, b, t, d), F32)
    return pl.pallas_call(
        _rwkv_proj_kernel,
        grid=(b, t // tt),
        in_specs=[tok,
                  pl.BlockSpec((1, HALO, d), lambda bi, ti: (bi, jnp.maximum(ti * step - 1, 0), 0)),
                  pl.BlockSpec((1, HALO, d), lambda bi, ti: (bi, jnp.minimum((ti + 1) * step, nh - 1), 0)),
                  _mod_spec(d), vec(1),
                  _const_spec((d, 4 * d)), _const_spec((d, 2 * LORA)), _const_spec((2 * LORA, d)),
                  _const_spec((d, 2 * LORA)), _const_spec((2 * LORA, d)),
                  vec(3), vec(2), vec(2), vec(2), vec(1), vec(1), vec(2),
                  _const_spec(down.shape), _const_spec(up.shape)],
        out_specs=[tok, tok, tok, tok, tok, tok, tok2, tok2],
        out_shape=[sdb, sdb, sdb, sdb, sd, sd, sd2, sd2],
        scratch_shapes=[pltpu.VMEM((tt + 2 * HALO, d), F32), pltpu.VMEM((tt + 2 * HALO, d), F32)],
        compiler_params=_params("parallel", "parallel"),
        name="rwkv_proj",
    )(x, x, x, mod, g_pre, w_rkvz, w1c, w2c, a1c, a2c, mu_rkv, mu_wa, w0, a0, k_k, k_a, r_k, down, up)


def _sconv_kernel(x_ref, mod_ref, g_ref, w_ref, scw_ref, o_ref, uext_ref, *, seg):
    tt, d = x_ref.shape[1], x_ref.shape[2]
    hb = _modulated(x_ref[0], mod_ref, g_ref).astype(BF16)
    proj = lambda i: jnp.dot(hb, w_ref[:, i * d:(i + 1) * d], preferred_element_type=F32)
    u = proj(1) * proj(2)
    zero = jnp.zeros((HALO, d), F32)
    uext_ref[0:HALO] = zero
    uext_ref[HALO:HALO + tt] = u
    uext_ref[HALO + tt:] = zero
    pos = lax.broadcasted_iota(jnp.int32, (tt, d), 0) & (seg - 1)
    prev = jnp.where(pos > 0, uext_ref[pl.ds(HALO - 1, tt), :], 0.0)
    nxt = jnp.where(pos < seg - 1, uext_ref[pl.ds(HALO + 1, tt), :], 0.0)
    conv = scw_ref[0:1, :] * prev + scw_ref[1:2, :] * u + scw_ref[2:3, :] * nxt
    o_ref[0] = proj(0) * conv * _silu(proj(3))


def _sconv(x, mod, g_pre, w_sc, sc_w, seg, tt):
    b, t, d = x.shape
    return pl.pallas_call(
        functools.partial(_sconv_kernel, seg=seg),
        grid=(b, t // tt),
        in_specs=[_tok_spec(tt, d), _mod_spec(d), _const_spec((1, d)), _const_spec((d, 4 * d)),
                  _const_spec((3, d))],
        out_specs=_tok_spec(tt, d),
        out_shape=jax.ShapeDtypeStruct((b, t, d), F32),
        scratch_shapes=[pltpu.VMEM((tt + 2 * HALO, d), F32)],
        compiler_params=_params("parallel", "parallel"),
        name="sconv",
    )(x, mod, g_pre, w_sc, sc_w)


TICK_EVERY = 4


def _each(f, *lists, tick=None):
    out = []
    for n, xs in enumerate(zip(*lists)):
        out.append(f(*xs))
        if tick is not None and n % TICK_EVERY == TICK_EVERY - 1:
            tick()
    return out


def _stack_heads(z):
    first = (lax.broadcasted_iota(jnp.int32, z.shape, 1) & HEAD) == 0
    return jnp.concatenate([jnp.where(first, z, 0.0), jnp.where(first, 0.0, z)], axis=0)


def _pair_dot(x, y):
    return _dot(x, _stack_heads(y))


def _unit_triangular_solves(mats, rhs, tick):
    rows = lax.broadcasted_iota(jnp.int32, mats[0].shape, 0)
    cols = lax.broadcasted_iota(jnp.int32, mats[0].shape, 1) & (HEAD - 1)
    eye = (rows == cols).astype(F32)
    blk16 = (rows >> 4) == (cols >> 4)
    n = mats[0].shape[1]
    dg = [jnp.where(blk16, a, 0.0) for a in mats]
    off = _each(lambda a, d: a - d, mats, dg)
    wide = lambda x, y: _pair_dot(x, jnp.concatenate([x, y], axis=1))
    d2 = _each(_pair_dot, dg, dg, tick=tick)
    tm = [eye + d for d in dg]
    both = _each(wide, d2, tm, tick=tick)
    d4, tm = [x[:, 0:n] for x in both], _each(lambda t0, x: t0 + x[:, n:], tm, both)
    both = _each(wide, d4, tm, tick=tick)
    d8, tm = [x[:, 0:n] for x in both], _each(lambda t0, x: t0 + x[:, n:], tm, both)
    tm = _each(lambda p, t0: t0 + _pair_dot(p, t0), d8, tm, tick=tick)
    nb = _each(_pair_dot, tm, off, tick=tick)
    x = _each(_pair_dot, tm, rhs, tick=tick)
    nb2 = _each(_pair_dot, nb, nb, tick=tick)
    x = _each(lambda p, x0: x0 + _pair_dot(p, x0), nb, x, tick=tick)
    return _each(lambda p, x0: x0 + _pair_dot(p, x0), nb2, x, tick=tick)


def _wkv_kernel(s0_ref, r_ref, k_ref, v_ref, kk_ref, lw_ref, a_ref, ka_ref, o_ref, s_out, s_ref, *stage_refs,
                nt, reverse):
    t = pl.program_id(2)
    tt, width = r_ref.shape[1], r_ref.shape[2]
    nc, npair = tt // CHUNK, width // PAIR
    order = list(range(nc - 1, -1, -1) if reverse else range(nc))
    units = [(p, j) for j in order for p in range(npair)]
    half = len(stage_refs) // 2
    stage = (stage_refs[:half], stage_refs[half:])

    @pl.when(t == 0)
    def _():
        s_ref[...] = s0_ref[0]
        for ref in stage[1][:-1]:
            ref[...] = jnp.zeros(ref.shape, ref.dtype)
        stage[1][-1][...] = jnp.ones(stage[1][-1].shape, F32)

    def recurrence(refs):
        reader_ref, w0_ref, vs_ref, v_ref2, arbk_ref, bkg_ref, gend_ref = refs
        same_head = ((lax.broadcasted_iota(jnp.int32, (PAIR, PAIR), 0) >> 6)
                     == (lax.broadcasted_iota(jnp.int32, (PAIR, PAIR), 1) >> 6))
        s = [s_ref[p] for p in range(npair)]
        for i, (p, j) in enumerate(units):
            sh = _dot_nt(reader_ref[i], s[p])
            yield
            u = sh[0:CHUNK] + w0_ref[i]
            uv_stacked = jnp.concatenate([_stack_heads(u).astype(BF16), vs_ref[i]], axis=0)
            o_ref[0, j * CHUNK:(j + 1) * CHUNK, p * PAIR:(p + 1) * PAIR] = (
                sh[CHUNK:] + jnp.dot(arbk_ref[i], uv_stacked, preferred_element_type=F32))
            upd = _dot_tn(jnp.concatenate([u.astype(BF16), v_ref2[i]], axis=0), bkg_ref[i])
            s[p] = s[p] * gend_ref[i] + jnp.where(same_head, upd, 0.0)
            yield
        for p in range(npair):
            s_ref[p] = s[p]

    @pl.when(t == nt)
    def _():
        for _ in recurrence(stage[(nt - 1) % 2]):
            pass
        for p in range(npair):
            s_out[0, p] = s_ref[p]

    for parity in range(2):
        @pl.when((t < nt) & (t % 2 == parity))
        def _():
            steps = recurrence(stage[1 - parity])
            _wkv_prepare(r_ref, k_ref, v_ref, kk_ref, lw_ref, a_ref, ka_ref, stage[parity], units, reverse,
                         lambda: next(steps, None))
            for _ in steps:
                pass


def _wkv_prepare(r_ref, k_ref, v_ref, kk_ref, lw_ref, a_ref, ka_ref, stage, units, reverse, tick):
    reader_ref, w0_ref, vs_ref, v_ref2, arbk_ref, bkg_ref, gend_ref = stage
    tt = r_ref.shape[1]
    ri = lax.broadcasted_iota(jnp.int32, (CHUNK, PAIR), 0)
    ci = lax.broadcasted_iota(jnp.int32, (CHUNK, PAIR), 1) & (HEAD - 1)
    rt, ct = lax.broadcasted_iota(jnp.int32, (tt, tt), 0), lax.broadcasted_iota(jnp.int32, (tt, tt), 1)
    same_chunk = (rt >> 6) == (ct >> 6)
    if reverse:
        strict, incl, tri = ri < ci, ri <= ci, (same_chunk & (rt <= ct)).astype(BF16)
    else:
        strict, incl, tri = ri > ci, ri >= ci, (same_chunk & (rt >= ct)).astype(BF16)

    lw, a, kk = lw_ref[0, 0], a_ref[0, 0], kk_ref[0].astype(F32)
    cum = _dot_exact_x(tri, lw)
    g_inv = jnp.exp(-cum)
    a_t = -kk * jnp.exp(cum - lw)
    r_t = r_ref[0].astype(F32) * jnp.exp(cum)
    b_t = kk * a * g_inv
    k_t = k_ref[0].astype(F32) * (1.0 + (a - 1.0) * ka_ref[...]) * g_inv
    v_t = v_ref[0].astype(F32)

    cut = lambda z: [z[j * CHUNK:(j + 1) * CHUNK, p * PAIR:(p + 1) * PAIR] for p, j in units]
    a1, r1, b1, k1, v1 = cut(a_t), cut(r_t), cut(b_t), cut(k_t), cut(v_t)
    end_row = lambda j: j * CHUNK if reverse else (j + 1) * CHUNK - 1
    g_end = [jnp.exp(cum[end_row(j):end_row(j) + 1, p * PAIR:(p + 1) * PAIR]) for p, j in units]
    ar = _each(lambda x, y: jnp.concatenate([x, y], axis=0), a1, r1)
    bk = _each(lambda x, y: jnp.concatenate([x, y], axis=0), b1, k1)
    sc = _each(lambda x, b, k: _dot_nt(x, jnp.concatenate([_stack_heads(b), _stack_heads(k)], axis=0)),
               ar, b1, k1, tick=tick)
    a_ab = [jnp.where(strict, x[0:CHUNK, 0:PAIR], 0.0) for x in sc]
    a_ak = [jnp.where(strict, x[0:CHUNK, PAIR:], 0.0) for x in sc]
    a_rbk = [jnp.concatenate([jnp.where(incl, x[CHUNK:, 0:PAIR], 0.0), jnp.where(incl, x[CHUNK:, PAIR:], 0.0)],
                             axis=1).astype(BF16) for x in sc]
    w_ak = _each(_pair_dot, a_ak, v1, tick=tick)
    rhs = _each(lambda x, w: jnp.concatenate([x, w], axis=1), a1, w_ak)
    pre = _unit_triangular_solves(a_ab, rhs, tick)
    for i in range(len(units)):
        reader_ref[i] = jnp.concatenate([pre[i][:, 0:PAIR], r1[i]], axis=0).astype(BF16)
        w0_ref[i] = pre[i][:, PAIR:]
        vs_ref[i] = _stack_heads(v1[i]).astype(BF16)
        v_ref2[i] = v1[i].astype(BF16)
        arbk_ref[i] = a_rbk[i]
        bkg_ref[i] = (bk[i] * g_end[i]).astype(BF16)
        gend_ref[i] = g_end[i]


def _wkv(s0, r, k, v, kk, lw, a, k_a, reverse, tt):
    b, t, d = r.shape
    nt = t // tt
    z = 1 if reverse else 0
    npair = min(SCAN_PAIRS, d // PAIR)
    width = npair * PAIR
    n_units = npair * (tt // CHUNK)
    tpos = (lambda ti: nt - 1 - ti) if reverse else (lambda ti: ti)
    t_in = lambda ti: tpos(jnp.minimum(ti, nt - 1))
    t_out = lambda ti: tpos(jnp.maximum(ti - 1, 0))
    tok = pl.BlockSpec((1, tt, width), lambda bi, p, ti: (bi, t_in(ti), p))
    tokz = pl.BlockSpec((1, 1, tt, width), lambda bi, p, ti: (z, bi, t_in(ti), p))
    tok_out = pl.BlockSpec((1, tt, width), lambda bi, p, ti: (bi, t_out(ti), p))
    st = pl.BlockSpec((1, npair, PAIR, PAIR), lambda bi, p, ti: (bi, p, 0, 0))
    return pl.pallas_call(
        functools.partial(_wkv_kernel, nt=nt, reverse=reverse),
        grid=(b, d // width, nt + 1),
        in_specs=[st, tok, tok, tok, tok, tokz, tokz, pl.BlockSpec((1, width), lambda bi, p, ti: (0, p))],
        out_specs=[tok_out, st],
        out_shape=[jax.ShapeDtypeStruct((b, t, d), F32), jax.ShapeDtypeStruct((b, d // PAIR, PAIR, PAIR), F32)],
        scratch_shapes=[pltpu.VMEM((npair, PAIR, PAIR), F32)] + 2 * [
            pltpu.VMEM((n_units, 2 * CHUNK, PAIR), BF16), pltpu.VMEM((n_units, CHUNK, PAIR), F32),
            pltpu.VMEM((n_units, 2 * CHUNK, PAIR), BF16), pltpu.VMEM((n_units, CHUNK, PAIR), BF16),
            pltpu.VMEM((n_units, CHUNK, 2 * PAIR), BF16), pltpu.VMEM((n_units, 2 * CHUNK, PAIR), BF16),
            pltpu.VMEM((n_units, 1, PAIR), F32)],
        compiler_params=_params("parallel", "parallel", "arbitrary"),
        name="wkv_bwd" if reverse else "wkv_fwd",
    )(s0, r, k, v, kk, lw, a, k_a)


def _even_out_kernel(x_ref, of_ref, ob_ref, bonus_ref, sza_ref, yb_ref, mod_ref, lnw_ref, lnb_ref, gp_ref,
                     w_ref, down_ref, up_ref, o_ref):
    d = x_ref.shape[2]
    down, up = down_ref[...], up_ref[...]
    o = of_ref[0] + ob_ref[0]
    dev = o - _head_sums(o, down, up) * (1.0 / HEAD)
    var = _head_sums(dev * dev, down, up) * (1.0 / HEAD)
    ya = (dev * lax.rsqrt(var + GN_EPS) * lnw_ref[...] + lnb_ref[...] + bonus_ref[0]) * sza_ref[0]
    y = _dot(ya, w_ref[0:d, :]) + _dot(yb_ref[0], w_ref[d:2 * d, :])
    o_ref[0] = x_ref[0] + mod_ref[0, 2:3, :] * _rms(y, gp_ref[...])


def _even_out(x, o_f, o_b, bonus, sza, ybg, mod, lnx_w, lnx_b, g_post, w_out, down, up, tt):
    b, t, d = x.shape
    tok = _tok_spec(tt, d)
    vec = _const_spec((1, d))
    return pl.pallas_call(
        _even_out_kernel,
        grid=(b, t // tt),
        in_specs=[tok, tok, tok, tok, tok, tok, _mod_spec(d), vec, vec, vec,
                  _const_spec((2 * d, d)), _const_spec(down.shape), _const_spec(up.shape)],
        out_specs=tok,
        out_shape=jax.ShapeDtypeStruct((b, t, d), F32),
        compiler_params=_params("parallel", "parallel"),
        name="even_out",
    )(x, o_f, o_b, bonus, sza, ybg, mod, lnx_w, lnx_b, g_post, w_out, down, up)


def _conformer_kernel(x_ref, mod_ref, g_ref, w_in_ref, dww_ref, dwb_ref, lnw_ref, lnb_ref, w_out_ref, gp_ref,
                      o_ref, pad_ref, conv_ref, shift_ref, *, seg):
    tt, d = x_ref.shape[1], x_ref.shape[2]
    e = dww_ref.shape[1]
    taps = dww_ref.shape[0]
    span = seg + 2 * CONV_PAD
    x = x_ref[0]
    hb = _modulated(x, mod_ref, g_ref).astype(BF16)
    proj = lambda i: jnp.dot(hb, w_in_ref[:, i * e:(i + 1) * e], preferred_element_type=F32)
    glu = proj(0) * _sigmoid(proj(1))
    zero = jnp.zeros((CONV_PAD, e), F32)
    for s in range(tt // seg):
        pad_ref[s * span:s * span + CONV_PAD] = zero
        pad_ref[s * span + CONV_PAD:s * span + CONV_PAD + seg] = glu[s * seg:(s + 1) * seg]
        pad_ref[s * span + CONV_PAD + seg:(s + 1) * span] = zero

    rows = min(seg, CHUNK)
    lead = CONV_PAD - taps // 2
    win = shift_ref.shape[1]
    for s in range(tt // seg):
        for r0 in range(0, seg, rows):
            for c0 in range(0, e, CONV_COLS):
                cols = pl.ds(c0, CONV_COLS)
                for phase in range(8):
                    shift_ref[phase] = pad_ref[pl.ds(s * span + r0 + phase, win), cols]
                acc = jnp.zeros((rows, CONV_COLS), F32)
                for j in range(taps):
                    q, phase = divmod(lead + j, 8)
                    acc = acc + dww_ref[j:j + 1, cols] * shift_ref[phase, pl.ds(8 * q, rows), :]
                conv_ref[pl.ds(s * seg + r0, rows), cols] = acc

    conv = conv_ref[...] + dwb_ref[...]
    dev = conv - jnp.mean(conv, axis=-1, keepdims=True)
    var = jnp.mean(dev * dev, axis=-1, keepdims=True)
    act = _silu(dev * lax.rsqrt(var + LN_EPS) * lnw_ref[...] + lnb_ref[...]) * _silu(proj(2))
    o_ref[0] = x + mod_ref[0, 2:3, :] * _rms(_dot(act, w_out_ref[...]), gp_ref[...])


def _conformer(x, mod, g_pre, w_in, dw_w, dw_b, ln_w, ln_b, w_out, g_post, seg, tt):
    b, t, d = x.shape
    taps, e = dw_w.shape
    assert taps // 2 <= CONV_PAD and tt % seg == 0
    rows = min(seg, CHUNK)
    win = rows + 8 * ((CONV_PAD + taps // 2) // 8)
    assert 7 + win <= rows + 2 * CONV_PAD
    tok = _tok_spec(tt, d)
    return pl.pallas_call(
        functools.partial(_conformer_kernel, seg=seg),
        grid=(b, t // tt),
        in_specs=[tok, _mod_spec(d), _const_spec((1, d)), _const_spec((d, 3 * e)), _const_spec((taps, e)),
                  _const_spec((1, e)), _const_spec((1, e)), _const_spec((1, e)), _const_spec((e, d)),
                  _const_spec((1, d))],
        out_specs=tok,
        out_shape=jax.ShapeDtypeStruct((b, t, d), F32),
        scratch_shapes=[pltpu.VMEM(((tt // seg) * (seg + 2 * CONV_PAD), e), F32), pltpu.VMEM((tt, e), F32),
                        pltpu.VMEM((8, win, CONV_COLS), F32)],
        compiler_params=_params("parallel", "parallel"),
        name="conformer",
    )(x, mod, g_pre, w_in, dw_w, dw_b, ln_w, ln_b, w_out, g_post)


def _conformer_tm_kernel(x_ref, mod_ref, g_ref, w_in_ref, dww_ref, dwb_ref, lnw_ref, lnb_ref, w_out_ref, gp_ref,
                         o_ref, pad_ref, conv_ref, gate_ref):
    seg, nseg, d = x_ref.shape
    n = seg * nseg
    taps, e = dww_ref.shape[0], dww_ref.shape[2]
    x = x_ref[...].reshape(n, d)
    hb = _modulated(x, mod_ref, g_ref).astype(BF16)
    proj = lambda lo, hi: jnp.dot(hb, w_in_ref[:, lo:hi], preferred_element_type=F32)
    zero = jnp.zeros((CONV_PAD, nseg, e), F32)
    pad_ref[0:CONV_PAD] = zero
    pad_ref[CONV_PAD + seg:] = zero
    for c0 in range(0, e, CONV_COLS):
        glu = proj(c0, c0 + CONV_COLS) * _sigmoid(proj(e + c0, e + c0 + CONV_COLS))
        pad_ref[CONV_PAD:CONV_PAD + seg, :, c0:c0 + CONV_COLS] = glu.reshape(seg, nseg, CONV_COLS)

    lead = CONV_PAD - taps // 2

    for c0 in range(0, e, CONV_COLS):
        cols = pl.ds(c0, CONV_COLS)
        for t0 in range(0, seg, 8):
            acc = jnp.zeros((8, nseg, CONV_COLS), F32)
            for j in range(taps):
                acc = acc + dww_ref[j, :, cols][None] * pad_ref[pl.ds(t0 + lead + j, 8), :, cols]
            conv_ref[pl.ds(t0, 8), :, cols] = acc
        gate_ref[:, c0:c0 + CONV_COLS] = _silu(proj(2 * e + c0, 2 * e + c0 + CONV_COLS))

    conv = conv_ref[...].reshape(n, e) + dwb_ref[...]
    dev = conv - jnp.mean(conv, axis=-1, keepdims=True)
    var = jnp.mean(dev * dev, axis=-1, keepdims=True)
    act = _silu(dev * lax.rsqrt(var + LN_EPS) * lnw_ref[...] + lnb_ref[...]) * gate_ref[...]
    out = x + mod_ref[0, 2:3, :] * _rms(_dot(act, w_out_ref[...]), gp_ref[...])
    o_ref[...] = out.reshape(seg, nseg, d)


def _conformer_tm(x, mod, g_pre, w_in, dw_w, dw_b, ln_w, ln_b, w_out, g_post, seg):
    b, t, d = x.shape
    taps, e = dw_w.shape
    nseg = 8
    groups = t // (seg * nseg)
    assert taps // 2 <= CONV_PAD and seg % 8 == 0 and groups * seg * nseg == t
    x5 = x.reshape(b, seg, groups, nseg, d)
    tile = pl.BlockSpec((None, seg, None, nseg, d), lambda bi, gi: (bi, 0, gi, 0, 0))
    out = pl.pallas_call(
        _conformer_tm_kernel,
        grid=(b, groups),
        in_specs=[tile, _mod_spec(d), _const_spec((1, d)), _const_spec((d, 3 * e)), _const_spec((taps, nseg, e)),
                  _const_spec((1, e)), _const_spec((1, e)), _const_spec((1, e)), _const_spec((e, d)),
                  _const_spec((1, d))],
        out_specs=tile,
        out_shape=jax.ShapeDtypeStruct(x5.shape, F32),
        scratch_shapes=[pltpu.VMEM((seg + 2 * CONV_PAD, nseg, e), F32), pltpu.VMEM((seg, nseg, e), F32),
                        pltpu.VMEM((seg * nseg, e), F32)],
        compiler_params=_params("parallel", "parallel"),
        name="conformer_tm",
    )(x5, mod, g_pre, w_in, jnp.broadcast_to(dw_w[:, None, :], (taps, nseg, e)), dw_b, ln_w, ln_b, w_out, g_post)
    return out.reshape(b, t, d)


def _to_col_major(h, rows):
    b, t, ch = h.shape
    return h.reshape(b, rows, GRID_W, ch).swapaxes(1, 2).reshape(b, t, ch)


def _to_row_major(h, rows):
    b, t, ch = h.shape
    return h.reshape(b, GRID_W, rows, ch).swapaxes(1, 2).reshape(b, t, ch)


def kernel(x, c, ctx, c_ctx, mod_w, mod_b, g_pre, g_post, ev_w_in, ev_w_out, ev_mu_rkv, ev_mu_wa, ev_w0, ev_w1, ev_w2, ev_a0, ev_a1, ev_a2, ev_k_k, ev_k_a, ev_r_k, ev_lnx_w, ev_lnx_b, ev_sc_w, od_w_in, od_dw_w, od_dw_b, od_ln_w, od_ln_b, od_w_out):
    b, t, d = x.shape
    n_ctx = ctx.shape[1]
    depth = mod_w.shape[0]
    rows = t // GRID_W
    assert d % PAIR == 0 and t % TOKEN_TILE == 0 and n_ctx % CHUNK == 0
    tt_ctx = min(TOKEN_TILE, n_ctx)

    mod_rows = -(-(b + 1) // 8) * 8
    c_all = jnp.zeros((mod_rows, d), F32).at[:b].set(c).at[b].set(c_ctx)
    mods = _modulation(c_all, mod_w, mod_b).reshape(depth, mod_rows, 3, d)
    head_cols = -(-(d // HEAD) // 128) * 128
    down = (jnp.arange(d)[:, None] // HEAD == jnp.arange(head_cols)[None, :]).astype(BF16)
    sums = (down, down.T)

    col_major = False
    for i in range(depth):
        j = i // 2
        last = i == depth - 1
        col = j % 2 == 1
        want_col_major = col if i % 2 == 0 else not col
        if want_col_major != col_major:
            x = _to_col_major(x, rows) if want_col_major else _to_row_major(x, rows)
            col_major = want_col_major
        seg = rows if col else GRID_W
        mod_x = mods[i, :b]
        mod_c = jnp.broadcast_to(mods[i, b], (b, 3, d))
        gpre, gpost = g_pre[i][None], g_post[i][None]
        if i % 2 == 0:
            w_in = ev_w_in[j].astype(BF16)
            w_rkvz, w_sc = w_in[:, :4 * d], w_in[:, 4 * d:]
            w_out = ev_w_out[j].astype(BF16)
            cat = lambda w: jnp.concatenate([w[0], w[1]], axis=1).astype(BF16)
            stk = lambda w: jnp.concatenate([w[0], w[1]], axis=0).astype(BF16)
            k_a = ev_k_a[j][None]
            proj_args = (gpre, w_rkvz, cat(ev_w1[j]), stk(ev_w2[j]), cat(ev_a1[j]), stk(ev_a2[j]), ev_mu_rkv[j],
                         ev_mu_wa[j], ev_w0[j], ev_a0[j], ev_k_k[j][None], k_a, ev_r_k[j].reshape(2, d), *sums)
            pc = _rwkv_proj(ctx, mod_c, *proj_args, tt_ctx)
            px = _rwkv_proj(x, mod_x, *proj_args, TOKEN_TILE)
            zero = jnp.zeros((b, d // PAIR, PAIR, PAIR), F32)
            scan = lambda p, s0, rev, tt: _wkv(s0, p[0], p[1], p[2], p[3], p[6], p[7], k_a, rev, tt)
            oc_f, s_f = scan(pc, zero, False, min(SCAN_TILE, n_ctx))
            oc_b, s_b = scan(pc, zero, True, min(SCAN_TILE, n_ctx))
            ox_f, _ = scan(px, s_f, False, SCAN_TILE)
            ox_b, _ = scan(px, s_b, True, SCAN_TILE)
            yb_x = _sconv(x, mod_x, gpre, w_sc, ev_sc_w[j], seg, TOKEN_TILE)
            out_args = (ev_lnx_w[j][None], ev_lnx_b[j][None], gpost, w_out, *sums)
            x = _even_out(x, ox_f, ox_b, px[5], px[4], yb_x, mod_x, *out_args, TOKEN_TILE)
            if not last:
                yb_c = _sconv(ctx, mod_c, gpre, w_sc, ev_sc_w[j], n_ctx, tt_ctx)
                ctx = _even_out(ctx, oc_f, oc_b, pc[5], pc[4], yb_c, mod_c, *out_args, tt_ctx)
        else:
            odd_args = (gpre, od_w_in[j].astype(BF16), od_dw_w[j], od_dw_b[j][None], od_ln_w[j][None],
                        od_ln_b[j][None], od_w_out[j].astype(BF16), gpost)
            if not last:
                ctx_new = _conformer(ctx, mod_c, *odd_args, n_ctx, tt_ctx)
            x = _conformer_tm(x, mod_x, *odd_args, seg)
            if not last:
                ctx = ctx_new
    if col_major:
        x = _to_row_major(x, rows)
    return x
```

```python
import functools

import jax
import jax.numpy as jnp
from jax import lax
from jax.experimental import pallas as pl
from jax.experimental.pallas import tpu as pltpu

F32 = jnp.float32
BF16 = jnp.bfloat16

GRID_W = 64
HEAD = 64
PAIR = 2 * HEAD
CHUNK = 64
LORA = 64
RMS_EPS = 1e-6
LN_EPS = 1e-5
GN_EPS = 64e-5
HALO = 8
TOKEN_TILE = 256
SCAN_TILE = 256
SCAN_PAIRS = 4
CONV_PAD = 16
CONV_COLS = 512
VMEM_LIMIT = 56 * 1024 * 1024


def _dot(a, b):
    return jnp.dot(a.astype(BF16), b.astype(BF16), preferred_element_type=F32)


def _dot_nt(a, b):
    return lax.dot_general(a.astype(BF16), b.astype(BF16), (((1,), (1,)), ((), ())),
                           preferred_element_type=F32)


def _dot_tn(a, b):
    return lax.dot_general(a.astype(BF16), b.astype(BF16), (((0,), (0,)), ((), ())),
                           preferred_element_type=F32)


def _split3(x):
    hi = x.astype(BF16)
    r1 = x - hi.astype(F32)
    mid = r1.astype(BF16)
    lo = (r1 - mid.astype(F32)).astype(BF16)
    return hi, mid, lo


def _dot_x_exact(x, e):
    hi, mid, lo = _split3(x)
    f = lambda p: jnp.dot(p, e, preferred_element_type=F32)
    return f(hi) + f(mid) + f(lo)


def _head_sums(x, down, up):
    def two_pass(y, e):
        hi = y.astype(BF16)
        mid = (y - hi.astype(F32)).astype(BF16)
        return jnp.dot(hi, e, preferred_element_type=F32) + jnp.dot(mid, e, preferred_element_type=F32)
    return two_pass(two_pass(x, down), up)


def _dot_exact_x(e, x):
    hi, mid, lo = _split3(x)
    f = lambda p: jnp.dot(e, p, preferred_element_type=F32)
    return f(hi) + f(mid) + f(lo)


def _sigmoid(x):
    return 1.0 / (1.0 + jnp.exp(-x))


def _silu(x):
    return x * _sigmoid(x)


def _rms(x, g):
    return x * lax.rsqrt(jnp.mean(x * x, axis=-1, keepdims=True) + RMS_EPS) * g


def _modulated(x, mod_ref, g_ref):
    return _rms(x, g_ref[...]) * (1.0 + mod_ref[0, 1:2, :]) + mod_ref[0, 0:1, :]


def _const_spec(shape):
    return pl.BlockSpec(shape, lambda *_: (0,) * len(shape), pipeline_mode=pl.Buffered(1))


def _tok_spec(tt, d):
    return pl.BlockSpec((1, tt, d), lambda b, t: (b, t, 0))


def _mod_spec(d):
    return pl.BlockSpec((1, 3, d), lambda b, t: (b, 0, 0))


def _mod_kernel(c_ref, w_ref, b_ref, o_ref):
    o_ref[0] = _dot(_silu(c_ref[...]), w_ref[0]) + b_ref[0]


def _modulation(c_all, mod_w, mod_b):
    depth, d, d3 = mod_w.shape
    rows = c_all.shape[0]
    return pl.pallas_call(
        _mod_kernel,
        grid=(depth, d3 // d),
        in_specs=[pl.BlockSpec((rows, d), lambda i, n: (0, 0)),
                  pl.BlockSpec((1, d, d), lambda i, n: (i, 0, n)),
                  pl.BlockSpec((1, 1, d), lambda i, n: (i, 0, n))],
        out_specs=pl.BlockSpec((1, rows, d), lambda i, n: (i, 0, n)),
        out_shape=jax.ShapeDtypeStruct((depth, rows, d3), F32),
        name="modulation",
    )(c_all, mod_w, mod_b.reshape(depth, 1, d3))


def _rwkv_proj_kernel(x_ref, xp_ref, xn_ref, mod_ref, g_ref, w_ref, w1_ref, w2_ref, a1_ref, a2_ref,
                      mu_rkv_ref, mu_wa_ref, w0_ref, a0_ref, kk_ref, ka_ref, rk_ref, down_ref, up_ref,
                      r_out, k_out, v_out, kkn_out, sza_out, bonus_out, lw_out, a_out,
                      hext_ref, pext_ref):
    t = pl.program_id(1)
    nt = pl.num_programs(1)
    tt, d = x_ref.shape[1], x_ref.shape[2]
    h = _modulated(x_ref[0], mod_ref, g_ref)
    hext_ref[0:HALO] = jnp.where(t > 0, _modulated(xp_ref[0], mod_ref, g_ref), 0.0)
    hext_ref[HALO:HALO + tt] = h
    hext_ref[HALO + tt:] = jnp.where(t < nt - 1, _modulated(xn_ref[0], mod_ref, g_ref), 0.0)

    def shifted(ref):
        return 0.5 * (ref[pl.ds(HALO - 1, tt), :] + ref[pl.ds(HALO + 1, tt), :])

    dh = shifted(hext_ref) - h
    lane = lax.broadcasted_iota(jnp.int32, (tt, 2 * LORA), 1)

    def lora(xin, down_ref, up_ref, act):
        mid = act(_dot(xin, down_ref[...]))
        up = up_ref[...]
        return (_dot(jnp.where(lane < LORA, mid, 0.0), up), _dot(jnp.where(lane >= LORA, mid, 0.0), up))

    lw = lora(h + dh * mu_wa_ref[0:1, :], w1_ref, w2_ref, jnp.tanh)
    la = lora(h + dh * mu_wa_ref[1:2, :], a1_ref, a2_ref, lambda z: z)
    a_dir = []
    for z in range(2):
        lw_out[z, 0] = -jnp.exp(-0.5) * _sigmoid(w0_ref[z:z + 1, :] + lw[z])
        a_z = _sigmoid(a0_ref[z:z + 1, :] + la[z])
        a_out[z, 0] = a_z
        a_dir.append(a_z)

    hext_b = hext_ref[...].astype(BF16)
    hb = h.astype(BF16)
    mixed = []
    for g in range(3):
        pext_ref[...] = jnp.dot(hext_b, w_ref[:, g * d:(g + 1) * d], preferred_element_type=F32)
        centre = pext_ref[pl.ds(HALO, tt), :]
        mixed.append(centre + (shifted(pext_ref) - centre) * mu_rkv_ref[g:g + 1, :])
    r, k, v = mixed
    r_out[0] = r
    k_out[0] = k
    v_out[0] = v
    sza_out[0] = _silu(jnp.dot(hb, w_ref[:, 3 * d:4 * d], preferred_element_type=F32))

    down, up = down_ref[...], up_ref[...]
    kq = k * kk_ref[...]
    kkn_out[0] = kq * lax.rsqrt(jnp.maximum(_head_sums(kq * kq, down, up), 1e-24))
    ka = ka_ref[...]
    bsum = r * k * ((1.0 + (a_dir[0] - 1.0) * ka) * rk_ref[0:1, :] + (1.0 + (a_dir[1] - 1.0) * ka) * rk_ref[1:2, :])
    bonus_out[0] = _head_sums(bsum, down, up) * v


def _rwkv_proj(x, mod, g_pre, w_rkvz, w1c, w2c, a1c, a2c, mu_rkv, mu_wa, w0, a0, k_k, k_a, r_k, down, up, tt):
    b, t, d = x.shape
    nh = t // HALO
    step = tt // HALO
    vec = lambda n: _const_spec((n, d))
    tok = _tok_spec(tt, d)
    tok2 = pl.BlockSpec((2, 1, tt, d), lambda bi, ti: (0, bi, ti, 0))
    sd = jax.ShapeDtypeStruct((b, t, d), F32)
    sd2 = jax.ShapeDtypeStruct((2, b, t, d), F32)
    return pl.pallas_call(
        _rwkv_proj_kernel,
        grid=(b, t // tt),
        in_specs=[tok,
                  pl.BlockSpec((1, HALO, d), lambda bi, ti: (bi, jnp.maximum(ti * step - 1, 0), 0)),
                  pl.BlockSpec((1, HALO, d), lambda bi, ti: (bi, jnp.minimum((ti + 1) * step, nh - 1), 0)),
                  _mod_spec(d), vec(1),
                  _const_spec((d, 4 * d)), _const_spec((d, 2 * LORA)), _const_spec((2 * LORA, d)),
                  _const_spec((d, 2 * LORA)), _const_spec((2 * LORA, d)),
                  vec(3), vec(2), vec(2), vec(2), vec(1), vec(1), vec(2),
                  _const_spec(down.shape), _const_spec(up.shape)],
        out_specs=[tok, tok, tok, tok, tok, tok, tok2, tok2],
        out_shape=[sd, sd, sd, sd, sd, sd, sd2, sd2],
        scratch_shapes=[pltpu.VMEM((tt + 2 * HALO, d), F32), pltpu.VMEM((tt + 2 * HALO, d), F32)],
        compiler_params=pltpu.CompilerParams(dimension_semantics=("parallel", "parallel"),
                                             vmem_limit_bytes=VMEM_LIMIT),
        name="rwkv_proj",
    )(x, x, x, mod, g_pre, w_rkvz, w1c, w2c, a1c, a2c, mu_rkv, mu_wa, w0, a0, k_k, k_a, r_k, down, up)


def _short_conv(x, mod_ref, g_ref, w_ref, scw_ref, uext_ref, seg):
    tt, d = x.shape
    hb = _modulated(x, mod_ref, g_ref).astype(BF16)
    proj = lambda i: jnp.dot(hb, w_ref[:, i * d:(i + 1) * d], preferred_element_type=F32)
    u = proj(1) * proj(2)
    zero = jnp.zeros((HALO, d), F32)
    uext_ref[0:HALO] = zero
    uext_ref[HALO:HALO + tt] = u
    uext_ref[HALO + tt:] = zero
    pos = lax.broadcasted_iota(jnp.int32, (tt, d), 0) & (seg - 1)
    prev = jnp.where(pos > 0, uext_ref[pl.ds(HALO - 1, tt), :], 0.0)
    nxt = jnp.where(pos < seg - 1, uext_ref[pl.ds(HALO + 1, tt), :], 0.0)
    conv = scw_ref[0:1, :] * prev + scw_ref[1:2, :] * u + scw_ref[2:3, :] * nxt
    return proj(0) * conv * _silu(proj(3))


TICK_EVERY = 4


def _each(f, *lists, tick=None):
    out = []
    for n, xs in enumerate(zip(*lists)):
        out.append(f(*xs))
        if tick is not None and n % TICK_EVERY == TICK_EVERY - 1:
            tick()
    return out


def _stack_heads(z):
    first = (lax.broadcasted_iota(jnp.int32, z.shape, 1) & HEAD) == 0
    return jnp.concatenate([jnp.where(first, z, 0.0), jnp.where(first, 0.0, z)], axis=0)


def _pair_dot(x, y):
    return _dot(x, _stack_heads(y))


def _unit_triangular_solves(mats, rhs, tick):
    rows = lax.broadcasted_iota(jnp.int32, mats[0].shape, 0)
    cols = lax.broadcasted_iota(jnp.int32, mats[0].shape, 1) & (HEAD - 1)
    eye = (rows == cols).astype(F32)
    blk16 = (rows >> 4) == (cols >> 4)
    n = mats[0].shape[1]
    dg = [jnp.where(blk16, a, 0.0) for a in mats]
    off = _each(lambda a, d: a - d, mats, dg)
    wide = lambda x, y: _pair_dot(x, jnp.concatenate([x, y], axis=1))
    d2 = _each(_pair_dot, dg, dg, tick=tick)
    tm = [eye + d for d in dg]
    both = _each(wide, d2, tm, tick=tick)
    d4, tm = [x[:, 0:n] for x in both], _each(lambda t0, x: t0 + x[:, n:], tm, both)
    both = _each(wide, d4, tm, tick=tick)
    d8, tm = [x[:, 0:n] for x in both], _each(lambda t0, x: t0 + x[:, n:], tm, both)
    tm = _each(lambda p, t0: t0 + _pair_dot(p, t0), d8, tm, tick=tick)
    nb = _each(_pair_dot, tm, off, tick=tick)
    x = _each(_pair_dot, tm, rhs, tick=tick)
    nb2 = _each(_pair_dot, nb, nb, tick=tick)
    x = _each(lambda p, x0: x0 + _pair_dot(p, x0), nb, x, tick=tick)
    return _each(lambda p, x0: x0 + _pair_dot(p, x0), nb2, x, tick=tick)


def _wkv_kernel(s0_ref, r_ref, k_ref, v_ref, kk_ref, lw_ref, a_ref, ka_ref, o_ref, s_out, s_ref, *stage_refs,
                nt, reverse):
    t = pl.program_id(2)
    tt, width = r_ref.shape[1], r_ref.shape[2]
    nc, npair = tt // CHUNK, width // PAIR
    order = list(range(nc - 1, -1, -1) if reverse else range(nc))
    units = [(p, j) for j in order for p in range(npair)]
    half = len(stage_refs) // 2
    stage = (stage_refs[:half], stage_refs[half:])

    @pl.when(t == 0)
    def _():
        s_ref[...] = s0_ref[0]
        for ref in stage[1][:-1]:
            ref[...] = jnp.zeros(ref.shape, ref.dtype)
        stage[1][-1][...] = jnp.ones(stage[1][-1].shape, F32)

    def recurrence(refs):
        reader_ref, w0_ref, vs_ref, v_ref2, arbk_ref, bkg_ref, gend_ref = refs
        same_head = ((lax.broadcasted_iota(jnp.int32, (PAIR, PAIR), 0) >> 6)
                     == (lax.broadcasted_iota(jnp.int32, (PAIR, PAIR), 1) >> 6))
        s = [s_ref[p] for p in range(npair)]
        for i, (p, j) in enumerate(units):
            sh = _dot_nt(reader_ref[i], s[p])
            yield
            u = sh[0:CHUNK] + w0_ref[i]
            uv_stacked = jnp.concatenate([_stack_heads(u).astype(BF16), vs_ref[i]], axis=0)
            o_ref[0, j * CHUNK:(j + 1) * CHUNK, p * PAIR:(p + 1) * PAIR] = (
                sh[CHUNK:] + jnp.dot(arbk_ref[i], uv_stacked, preferred_element_type=F32))
            upd = _dot_tn(jnp.concatenate([u.astype(BF16), v_ref2[i]], axis=0), bkg_ref[i])
            s[p] = s[p] * gend_ref[i] + jnp.where(same_head, upd, 0.0)
            yield
        for p in range(npair):
            s_ref[p] = s[p]

    @pl.when(t == nt)
    def _():
        for _ in recurrence(stage[(nt - 1) % 2]):
            pass
        for p in range(npair):
            s_out[0, p] = s_ref[p]

    for parity in range(2):
        @pl.when((t < nt) & (t % 2 == parity))
        def _():
            steps = recurrence(stage[1 - parity])
            _wkv_prepare(r_ref, k_ref, v_ref, kk_ref, lw_ref, a_ref, ka_ref, stage[parity], units, reverse,
                         lambda: next(steps, None))
            for _ in steps:
                pass


def _wkv_prepare(r_ref, k_ref, v_ref, kk_ref, lw_ref, a_ref, ka_ref, stage, units, reverse, tick):
    reader_ref, w0_ref, vs_ref, v_ref2, arbk_ref, bkg_ref, gend_ref = stage
    tt = r_ref.shape[1]
    ri = lax.broadcasted_iota(jnp.int32, (CHUNK, PAIR), 0)
    ci = lax.broadcasted_iota(jnp.int32, (CHUNK, PAIR), 1) & (HEAD - 1)
    rt, ct = lax.broadcasted_iota(jnp.int32, (tt, tt), 0), lax.broadcasted_iota(jnp.int32, (tt, tt), 1)
    same_chunk = (rt >> 6) == (ct >> 6)
    if reverse:
        strict, incl, tri = ri < ci, ri <= ci, (same_chunk & (rt <= ct)).astype(BF16)
    else:
        strict, incl, tri = ri > ci, ri >= ci, (same_chunk & (rt >= ct)).astype(BF16)

    lw, a, kk = lw_ref[0, 0], a_ref[0, 0], kk_ref[0]
    cum = _dot_exact_x(tri, lw)
    g_inv = jnp.exp(-cum)
    a_t = -kk * jnp.exp(cum - lw)
    r_t = r_ref[0] * jnp.exp(cum)
    b_t = kk * a * g_inv
    k_t = k_ref[0] * (1.0 + (a - 1.0) * ka_ref[...]) * g_inv
    v_t = v_ref[0]

    cut = lambda z: [z[j * CHUNK:(j + 1) * CHUNK, p * PAIR:(p + 1) * PAIR] for p, j in units]
    a1, r1, b1, k1, v1 = cut(a_t), cut(r_t), cut(b_t), cut(k_t), cut(v_t)
    end_row = lambda j: j * CHUNK if reverse else (j + 1) * CHUNK - 1
    g_end = [jnp.exp(cum[end_row(j):end_row(j) + 1, p * PAIR:(p + 1) * PAIR]) for p, j in units]
    ar = _each(lambda x, y: jnp.concatenate([x, y], axis=0), a1, r1)
    bk = _each(lambda x, y: jnp.concatenate([x, y], axis=0), b1, k1)
    sc = _each(lambda x, b, k: _dot_nt(x, jnp.concatenate([_stack_heads(b), _stack_heads(k)], axis=0)),
               ar, b1, k1, tick=tick)
    a_ab = [jnp.where(strict, x[0:CHUNK, 0:PAIR], 0.0) for x in sc]
    a_ak = [jnp.where(strict, x[0:CHUNK, PAIR:], 0.0) for x in sc]
    a_rbk = [jnp.concatenate([jnp.where(incl, x[CHUNK:, 0:PAIR], 0.0), jnp.where(incl, x[CHUNK:, PAIR:], 0.0)],
                             axis=1).astype(BF16) for x in sc]
    w_ak = _each(_pair_dot, a_ak, v1, tick=tick)
    rhs = _each(lambda x, w: jnp.concatenate([x, w], axis=1), a1, w_ak)
    pre = _unit_triangular_solves(a_ab, rhs, tick)
    for i in range(len(units)):
        reader_ref[i] = jnp.concatenate([pre[i][:, 0:PAIR], r1[i]], axis=0).astype(BF16)
        w0_ref[i] = pre[i][:, PAIR:]
        vs_ref[i] = _stack_heads(v1[i]).astype(BF16)
        v_ref2[i] = v1[i].astype(BF16)
        arbk_ref[i] = a_rbk[i]
        bkg_ref[i] = (bk[i] * g_end[i]).astype(BF16)
        gend_ref[i] = g_end[i]


def _wkv(s0, r, k, v, kk, lw, a, k_a, reverse, tt):
    b, t, d = r.shape
    nt = t // tt
    z = 1 if reverse else 0
    npair = min(SCAN_PAIRS, d // PAIR)
    width = npair * PAIR
    n_units = npair * (tt // CHUNK)
    tpos = (lambda ti: nt - 1 - ti) if reverse else (lambda ti: ti)
    t_in = lambda ti: tpos(jnp.minimum(ti, nt - 1))
    t_out = lambda ti: tpos(jnp.maximum(ti - 1, 0))
    tok = pl.BlockSpec((1, tt, width), lambda bi, p, ti: (bi, t_in(ti), p))
    tokz = pl.BlockSpec((1, 1, tt, width), lambda bi, p, ti: (z, bi, t_in(ti), p))
    tok_out = pl.BlockSpec((1, tt, width), lambda bi, p, ti: (bi, t_out(ti), p))
    st = pl.BlockSpec((1, npair, PAIR, PAIR), lambda bi, p, ti: (bi, p, 0, 0))
    return pl.pallas_call(
        functools.partial(_wkv_kernel, nt=nt, reverse=reverse),
        grid=(b, d // width, nt + 1),
        in_specs=[st, tok, tok, tok, tok, tokz, tokz, pl.BlockSpec((1, width), lambda bi, p, ti: (0, p))],
        out_specs=[tok_out, st],
        out_shape=[jax.ShapeDtypeStruct((b, t, d), F32), jax.ShapeDtypeStruct((b, d // PAIR, PAIR, PAIR), F32)],
        scratch_shapes=[pltpu.VMEM((npair, PAIR, PAIR), F32)] + 2 * [
            pltpu.VMEM((n_units, 2 * CHUNK, PAIR), BF16), pltpu.VMEM((n_units, CHUNK, PAIR), F32),
            pltpu.VMEM((n_units, 2 * CHUNK, PAIR), BF16), pltpu.VMEM((n_units, CHUNK, PAIR), BF16),
            pltpu.VMEM((n_units, CHUNK, 2 * PAIR), BF16), pltpu.VMEM((n_units, 2 * CHUNK, PAIR), BF16),
            pltpu.VMEM((n_units, 1, PAIR), F32)],
        compiler_params=pltpu.CompilerParams(dimension_semantics=("parallel", "parallel", "arbitrary"),
                                             vmem_limit_bytes=VMEM_LIMIT),
        name="wkv_bwd" if reverse else "wkv_fwd",
    )(s0, r, k, v, kk, lw, a, k_a)


def _even_out_kernel(x_ref, of_ref, ob_ref, bonus_ref, sza_ref, mod_ref, g_ref, wsc_ref, scw_ref, lnw_ref, lnb_ref,
                     gp_ref, w_ref, down_ref, up_ref, o_ref, uext_ref, *, seg):
    d = x_ref.shape[2]
    yb = _short_conv(x_ref[0], mod_ref, g_ref, wsc_ref, scw_ref, uext_ref, seg)
    down, up = down_ref[...], up_ref[...]
    o = of_ref[0] + ob_ref[0]
    dev = o - _head_sums(o, down, up) * (1.0 / HEAD)
    var = _head_sums(dev * dev, down, up) * (1.0 / HEAD)
    ya = (dev * lax.rsqrt(var + GN_EPS) * lnw_ref[...] + lnb_ref[...] + bonus_ref[0]) * sza_ref[0]
    y = _dot(ya, w_ref[0:d, :]) + _dot(yb, w_ref[d:2 * d, :])
    o_ref[0] = x_ref[0] + mod_ref[0, 2:3, :] * _rms(y, gp_ref[...])


def _even_out(x, o_f, o_b, bonus, sza, mod, g_pre, w_sc, sc_w, lnx_w, lnx_b, g_post, w_out, down, up, seg, tt):
    b, t, d = x.shape
    tok = _tok_spec(tt, d)
    vec = _const_spec((1, d))
    return pl.pallas_call(
        functools.partial(_even_out_kernel, seg=seg),
        grid=(b, t // tt),
        in_specs=[tok, tok, tok, tok, tok, _mod_spec(d), vec, _const_spec((d, 4 * d)), _const_spec((3, d)),
                  vec, vec, vec, _const_spec((2 * d, d)), _const_spec(down.shape), _const_spec(up.shape)],
        out_specs=tok,
        out_shape=jax.ShapeDtypeStruct((b, t, d), F32),
        scratch_shapes=[pltpu.VMEM((tt + 2 * HALO, d), F32)],
        compiler_params=pltpu.CompilerParams(dimension_semantics=("parallel", "parallel"),
                                             vmem_limit_bytes=VMEM_LIMIT),
        name="even_out",
    )(x, o_f, o_b, bonus, sza, mod, g_pre, w_sc, sc_w, lnx_w, lnx_b, g_post, w_out, down, up)


def _conformer_kernel(x_ref, mod_ref, g_ref, w_in_ref, dww_ref, dwb_ref, lnw_ref, lnb_ref, w_out_ref, gp_ref,
                      o_ref, pad_ref, conv_ref, shift_ref, *, seg):
    tt, d = x_ref.shape[1], x_ref.shape[2]
    e = dww_ref.shape[1]
    taps = dww_ref.shape[0]
    span = seg + 2 * CONV_PAD
    x = x_ref[0]
    hb = _modulated(x, mod_ref, g_ref).astype(BF16)
    proj = lambda i: jnp.dot(hb, w_in_ref[:, i * e:(i + 1) * e], preferred_element_type=F32)
    glu = proj(0) * _sigmoid(proj(1))
    zero = jnp.zeros((CONV_PAD, e), F32)
    for s in range(tt // seg):
        pad_ref[s * span:s * span + CONV_PAD] = zero
        pad_ref[s * span + CONV_PAD:s * span + CONV_PAD + seg] = glu[s * seg:(s + 1) * seg]
        pad_ref[s * span + CONV_PAD + seg:(s + 1) * span] = zero

    rows = min(seg, CHUNK)
    lead = CONV_PAD - taps // 2
    win = shift_ref.shape[1]
    for s in range(tt // seg):
        for r0 in range(0, seg, rows):
            for c0 in range(0, e, CONV_COLS):
                cols = pl.ds(c0, CONV_COLS)
                for phase in range(8):
                    shift_ref[phase] = pad_ref[pl.ds(s * span + r0 + phase, win), cols]
                acc = jnp.zeros((rows, CONV_COLS), F32)
                for j in range(taps):
                    q, phase = divmod(lead + j, 8)
                    acc = acc + dww_ref[j:j + 1, cols] * shift_ref[phase, pl.ds(8 * q, rows), :]
                conv_ref[pl.ds(s * seg + r0, rows), cols] = acc

    conv = conv_ref[...] + dwb_ref[...]
    dev = conv - jnp.mean(conv, axis=-1, keepdims=True)
    var = jnp.mean(dev * dev, axis=-1, keepdims=True)
    act = _silu(dev * lax.rsqrt(var + LN_EPS) * lnw_ref[...] + lnb_ref[...]) * _silu(proj(2))
    o_ref[0] = x + mod_ref[0, 2:3, :] * _rms(_dot(act, w_out_ref[...]), gp_ref[...])


def _conformer(x, mod, g_pre, w_in, dw_w, dw_b, ln_w, ln_b, w_out, g_post, seg, tt):
    b, t, d = x.shape
    taps, e = dw_w.shape
    assert taps // 2 <= CONV_PAD and tt % seg == 0
    rows = min(seg, CHUNK)
    win = rows + 8 * ((CONV_PAD + taps // 2) // 8)
    assert 7 + win <= rows + 2 * CONV_PAD
    tok = _tok_spec(tt, d)
    return pl.pallas_call(
        functools.partial(_conformer_kernel, seg=seg),
        grid=(b, t // tt),
        in_specs=[tok, _mod_spec(d), _const_spec((1, d)), _const_spec((d, 3 * e)), _const_spec((taps, e)),
                  _const_spec((1, e)), _const_spec((1, e)), _const_spec((1, e)), _const_spec((e, d)),
                  _const_spec((1, d))],
        out_specs=tok,
        out_shape=jax.ShapeDtypeStruct((b, t, d), F32),
        scratch_shapes=[pltpu.VMEM(((tt // seg) * (seg + 2 * CONV_PAD), e), F32), pltpu.VMEM((tt, e), F32),
                        pltpu.VMEM((8, win, CONV_COLS), F32)],
        compiler_params=pltpu.CompilerParams(dimension_semantics=("parallel", "parallel"),
                                             vmem_limit_bytes=VMEM_LIMIT),
        name="conformer",
    )(x, mod, g_pre, w_in, dw_w, dw_b, ln_w, ln_b, w_out, g_post)


def _conformer_tm_kernel(x_ref, mod_ref, g_ref, w_in_ref, dww_ref, dwb_ref, lnw_ref, lnb_ref, w_out_ref, gp_ref,
                         o_ref, pad_ref, conv_ref, gate_ref):
    seg, nseg, d = x_ref.shape
    n = seg * nseg
    taps, e = dww_ref.shape[0], dww_ref.shape[2]
    x = x_ref[...].reshape(n, d)
    hb = _modulated(x, mod_ref, g_ref).astype(BF16)
    proj = lambda lo, hi: jnp.dot(hb, w_in_ref[:, lo:hi], preferred_element_type=F32)
    zero = jnp.zeros((CONV_PAD, nseg, e), F32)
    pad_ref[0:CONV_PAD] = zero
    pad_ref[CONV_PAD + seg:] = zero
    for c0 in range(0, e, CONV_COLS):
        glu = proj(c0, c0 + CONV_COLS) * _sigmoid(proj(e + c0, e + c0 + CONV_COLS))
        pad_ref[CONV_PAD:CONV_PAD + seg, :, c0:c0 + CONV_COLS] = glu.reshape(seg, nseg, CONV_COLS)

    lead = CONV_PAD - taps // 2

    for c0 in range(0, e, CONV_COLS):
        cols = pl.ds(c0, CONV_COLS)
        for t0 in range(0, seg, 8):
            acc = jnp.zeros((8, nseg, CONV_COLS), F32)
            for j in range(taps):
                acc = acc + dww_ref[j, :, cols][None] * pad_ref[pl.ds(t0 + lead + j, 8), :, cols]
            conv_ref[pl.ds(t0, 8), :, cols] = acc
        gate_ref[:, c0:c0 + CONV_COLS] = _silu(proj(2 * e + c0, 2 * e + c0 + CONV_COLS))

    conv = conv_ref[...].reshape(n, e) + dwb_ref[...]
    dev = conv - jnp.mean(conv, axis=-1, keepdims=True)
    var = jnp.mean(dev * dev, axis=-1, keepdims=True)
    act = _silu(dev * lax.rsqrt(var + LN_EPS) * lnw_ref[...] + lnb_ref[...]) * gate_ref[...]
    out = x + mod_ref[0, 2:3, :] * _rms(_dot(act, w_out_ref[...]), gp_ref[...])
    o_ref[...] = out.reshape(seg, nseg, d)


def _conformer_tm(x, mod, g_pre, w_in, dw_w, dw_b, ln_w, ln_b, w_out, g_post, seg):
    b, t, d = x.shape
    taps, e = dw_w.shape
    nseg = 8
    groups = t // (seg * nseg)
    assert taps // 2 <= CONV_PAD and seg % 8 == 0 and groups * seg * nseg == t
    x5 = x.reshape(b, seg, groups, nseg, d)
    tile = pl.BlockSpec((None, seg, None, nseg, d), lambda bi, gi: (bi, 0, gi, 0, 0))
    out = pl.pallas_call(
        _conformer_tm_kernel,
        grid=(b, groups),
        in_specs=[tile, _mod_spec(d), _const_spec((1, d)), _const_spec((d, 3 * e)), _const_spec((taps, nseg, e)),
                  _const_spec((1, e)), _const_spec((1, e)), _const_spec((1, e)), _const_spec((e, d)),
                  _const_spec((1, d))],
        out_specs=tile,
        out_shape=jax.ShapeDtypeStruct(x5.shape, F32),
        scratch_shapes=[pltpu.VMEM((seg + 2 * CONV_PAD, nseg, e), F32), pltpu.VMEM((seg, nseg, e), F32),
                        pltpu.VMEM((seg * nseg, e), F32)],
        compiler_params=pltpu.CompilerParams(dimension_semantics=("parallel", "parallel"),
                                             vmem_limit_bytes=VMEM_LIMIT),
        name="conformer_tm",
    )(x5, mod, g_pre, w_in, jnp.broadcast_to(dw_w[:, None, :], (taps, nseg, e)), dw_b, ln_w, ln_b, w_out, g_post)
    return out.reshape(b, t, d)


def _to_col_major(h, rows):
    b, t, ch = h.shape
    return h.reshape(b, rows, GRID_W, ch).swapaxes(1, 2).reshape(b, t, ch)


def _to_row_major(h, rows):
    b, t, ch = h.shape
    return h.reshape(b, GRID_W, rows, ch).swapaxes(1, 2).reshape(b, t, ch)


def kernel(x, c, ctx, c_ctx, mod_w, mod_b, g_pre, g_post, ev_w_in, ev_w_out, ev_mu_rkv, ev_mu_wa, ev_w0, ev_w1, ev_w2, ev_a0, ev_a1, ev_a2, ev_k_k, ev_k_a, ev_r_k, ev_lnx_w, ev_lnx_b, ev_sc_w, od_w_in, od_dw_w, od_dw_b, od_ln_w, od_ln_b, od_w_out):
    b, t, d = x.shape
    n_ctx = ctx.shape[1]
    depth = mod_w.shape[0]
    rows = t // GRID_W
    assert d % PAIR == 0 and t % TOKEN_TILE == 0 and n_ctx % CHUNK == 0
    tt_ctx = min(TOKEN_TILE, n_ctx)

    mod_rows = -(-(b + 1) // 8) * 8
    c_all = jnp.zeros((mod_rows, d), F32).at[:b].set(c).at[b].set(c_ctx)
    mods = _modulation(c_all, mod_w, mod_b).reshape(depth, mod_rows, 3, d)
    head_cols = -(-(d // HEAD) // 128) * 128
    down = (jnp.arange(d)[:, None] // HEAD == jnp.arange(head_cols)[None, :]).astype(BF16)
    sums = (down, down.T)

    col_major = False
    for i in range(depth):
        j = i // 2
        last = i == depth - 1
        col = j % 2 == 1
        want_col_major = col if i % 2 == 0 else not col
        if want_col_major != col_major:
            x = _to_col_major(x, rows) if want_col_major else _to_row_major(x, rows)
            col_major = want_col_major
        seg = rows if col else GRID_W
        mod_x = mods[i, :b]
        mod_c = jnp.broadcast_to(mods[i, b], (b, 3, d))
        gpre, gpost = g_pre[i][None], g_post[i][None]
        if i % 2 == 0:
            w_in = ev_w_in[j].astype(BF16)
            w_rkvz, w_sc = w_in[:, :4 * d], w_in[:, 4 * d:]
            w_out = ev_w_out[j].astype(BF16)
            cat = lambda w: jnp.concatenate([w[0], w[1]], axis=1).astype(BF16)
            stk = lambda w: jnp.concatenate([w[0], w[1]], axis=0).astype(BF16)
            k_a = ev_k_a[j][None]
            proj_args = (gpre, w_rkvz, cat(ev_w1[j]), stk(ev_w2[j]), cat(ev_a1[j]), stk(ev_a2[j]), ev_mu_rkv[j],
                         ev_mu_wa[j], ev_w0[j], ev_a0[j], ev_k_k[j][None], k_a, ev_r_k[j].reshape(2, d), *sums)
            pc = _rwkv_proj(ctx, mod_c, *proj_args, tt_ctx)
            px = _rwkv_proj(x, mod_x, *proj_args, TOKEN_TILE)
            zero = jnp.zeros((b, d // PAIR, PAIR, PAIR), F32)
            scan = lambda p, s0, rev, tt: _wkv(s0, p[0], p[1], p[2], p[3], p[6], p[7], k_a, rev, tt)
            oc_f, s_f = scan(pc, zero, False, min(SCAN_TILE, n_ctx))
            oc_b, s_b = scan(pc, zero, True, min(SCAN_TILE, n_ctx))
            ox_f, _ = scan(px, s_f, False, SCAN_TILE)
            ox_b, _ = scan(px, s_b, True, SCAN_TILE)
            out_args = (gpre, w_sc, ev_sc_w[j], ev_lnx_w[j][None], ev_lnx_b[j][None], gpost, w_out, *sums)
            x = _even_out(x, ox_f, ox_b, px[5], px[4], mod_x, *out_args, seg, TOKEN_TILE)
            if not last:
                ctx = _even_out(ctx, oc_f, oc_b, pc[5], pc[4], mod_c, *out_args, n_ctx, tt_ctx)
        else:
            odd_args = (gpre, od_w_in[j].astype(BF16), od_dw_w[j], od_dw_b[j][None], od_ln_w[j][None],
                        od_ln_b[j][None], od_w_out[j].astype(BF16), gpost)
            if not last:
                ctx_new = _conformer(ctx, mod_c, *odd_args, n_ctx, tt_ctx)
            x = _conformer_tm(x, mod_x, *odd_args, seg)
            if not last:
                ctx = ctx_new
    if col_major:
        x = _to_row_major(x, rows)
    return x
```

```python
import functools

import jax
import jax.numpy as jnp
from jax import lax
from jax.experimental import pallas as pl
from jax.experimental.pallas import tpu as pltpu

F32 = jnp.float32
BF16 = jnp.bfloat16

GRID_W = 64
HEAD = 64
PAIR = 2 * HEAD
CHUNK = 64
LORA = 64
RMS_EPS = 1e-6
LN_EPS = 1e-5
GN_EPS = 64e-5
HALO = 8
TOKEN_TILE = 256
SCAN_TILE = 256
SCAN_PAIRS = 4
CONV_PAD = 16
CONV_COLS = 512
VMEM_LIMIT = 56 * 1024 * 1024


def _dot(a, b):
    return jnp.dot(a.astype(BF16), b.astype(BF16), preferred_element_type=F32)


def _dot_nt(a, b):
    return lax.dot_general(a.astype(BF16), b.astype(BF16), (((1,), (1,)), ((), ())),
                           preferred_element_type=F32)


def _dot_tn(a, b):
    return lax.dot_general(a.astype(BF16), b.astype(BF16), (((0,), (0,)), ((), ())),
                           preferred_element_type=F32)


def _split3(x):
    hi = x.astype(BF16)
    r1 = x - hi.astype(F32)
    mid = r1.astype(BF16)
    lo = (r1 - mid.astype(F32)).astype(BF16)
    return hi, mid, lo


def _dot_x_exact(x, e):
    hi, mid, lo = _split3(x)
    f = lambda p: jnp.dot(p, e, preferred_element_type=F32)
    return f(hi) + f(mid) + f(lo)


def _head_sums(x, down, up):
    def two_pass(y, e):
        hi = y.astype(BF16)
        mid = (y - hi.astype(F32)).astype(BF16)
        return jnp.dot(hi, e, preferred_element_type=F32) + jnp.dot(mid, e, preferred_element_type=F32)
    return two_pass(two_pass(x, down), up)


def _dot_exact_x(e, x):
    hi, mid, lo = _split3(x)
    f = lambda p: jnp.dot(e, p, preferred_element_type=F32)
    return f(hi) + f(mid) + f(lo)


def _sigmoid(x):
    return 1.0 / (1.0 + jnp.exp(-x))


def _silu(x):
    return x * _sigmoid(x)


def _rms(x, g):
    return x * lax.rsqrt(jnp.mean(x * x, axis=-1, keepdims=True) + RMS_EPS) * g


def _modulated(x, mod_ref, g_ref):
    return _rms(x, g_ref[...]) * (1.0 + mod_ref[0, 1:2, :]) + mod_ref[0, 0:1, :]


def _const_spec(shape):
    return pl.BlockSpec(shape, lambda *_: (0,) * len(shape), pipeline_mode=pl.Buffered(1))


def _tok_spec(tt, d):
    return pl.BlockSpec((1, tt, d), lambda b, t: (b, t, 0))


def _mod_spec(d):
    return pl.BlockSpec((1, 3, d), lambda b, t: (b, 0, 0))


def _mod_kernel(c_ref, w_ref, b_ref, o_ref):
    o_ref[0] = _dot(_silu(c_ref[...]), w_ref[0]) + b_ref[0]


def _modulation(c_all, mod_w, mod_b):
    depth, d, d3 = mod_w.shape
    rows = c_all.shape[0]
    return pl.pallas_call(
        _mod_kernel,
        grid=(depth, d3 // d),
        in_specs=[pl.BlockSpec((rows, d), lambda i, n: (0, 0)),
                  pl.BlockSpec((1, d, d), lambda i, n: (i, 0, n)),
                  pl.BlockSpec((1, 1, d), lambda i, n: (i, 0, n))],
        out_specs=pl.BlockSpec((1, rows, d), lambda i, n: (i, 0, n)),
        out_shape=jax.ShapeDtypeStruct((depth, rows, d3), F32),
        name="modulation",
    )(c_all, mod_w, mod_b.reshape(depth, 1, d3))


def _rwkv_proj_kernel(x_ref, xp_ref, xn_ref, mod_ref, g_ref, w_ref, w1_ref, w2_ref, a1_ref, a2_ref,
                      mu_rkv_ref, mu_wa_ref, w0_ref, a0_ref, kk_ref, ka_ref, rk_ref, down_ref, up_ref,
                      r_out, k_out, v_out, kkn_out, sza_out, bonus_out, lw_out, a_out,
                      hext_ref, pext_ref):
    t = pl.program_id(1)
    nt = pl.num_programs(1)
    tt, d = x_ref.shape[1], x_ref.shape[2]
    h = _modulated(x_ref[0], mod_ref, g_ref)
    hext_ref[0:HALO] = jnp.where(t > 0, _modulated(xp_ref[0], mod_ref, g_ref), 0.0)
    hext_ref[HALO:HALO + tt] = h
    hext_ref[HALO + tt:] = jnp.where(t < nt - 1, _modulated(xn_ref[0], mod_ref, g_ref), 0.0)

    def shifted(ref):
        return 0.5 * (ref[pl.ds(HALO - 1, tt), :] + ref[pl.ds(HALO + 1, tt), :])

    dh = shifted(hext_ref) - h
    lane = lax.broadcasted_iota(jnp.int32, (tt, 2 * LORA), 1)

    def lora(xin, down_ref, up_ref, act):
        mid = act(_dot(xin, down_ref[...]))
        up = up_ref[...]
        return (_dot(jnp.where(lane < LORA, mid, 0.0), up), _dot(jnp.where(lane >= LORA, mid, 0.0), up))

    lw = lora(h + dh * mu_wa_ref[0:1, :], w1_ref, w2_ref, jnp.tanh)
    la = lora(h + dh * mu_wa_ref[1:2, :], a1_ref, a2_ref, lambda z: z)
    a_dir = []
    for z in range(2):
        lw_out[z, 0] = -jnp.exp(-0.5) * _sigmoid(w0_ref[z:z + 1, :] + lw[z])
        a_z = _sigmoid(a0_ref[z:z + 1, :] + la[z])
        a_out[z, 0] = a_z
        a_dir.append(a_z)

    hext_b = hext_ref[...].astype(BF16)
    hb = h.astype(BF16)
    mixed = []
    for g in range(3):
        pext_ref[...] = jnp.dot(hext_b, w_ref[:, g * d:(g + 1) * d], preferred_element_type=F32)
        centre = pext_ref[pl.ds(HALO, tt), :]
        mixed.append(centre + (shifted(pext_ref) - centre) * mu_rkv_ref[g:g + 1, :])
    r, k, v = mixed
    r_out[0] = r
    k_out[0] = k
    v_out[0] = v
    sza_out[0] = _silu(jnp.dot(hb, w_ref[:, 3 * d:4 * d], preferred_element_type=F32))

    down, up = down_ref[...], up_ref[...]
    kq = k * kk_ref[...]
    kkn_out[0] = kq * lax.rsqrt(jnp.maximum(_head_sums(kq * kq, down, up), 1e-24))
    ka = ka_ref[...]
    bsum = r * k * ((1.0 + (a_dir[0] - 1.0) * ka) * rk_ref[0:1, :] + (1.0 + (a_dir[1] - 1.0) * ka) * rk_ref[1:2, :])
    bonus_out[0] = _head_sums(bsum, down, up) * v


def _rwkv_proj(x, mod, g_pre, w_rkvz, w1c, w2c, a1c, a2c, mu_rkv, mu_wa, w0, a0, k_k, k_a, r_k, down, up, tt):
    b, t, d = x.shape
    nh = t // HALO
    step = tt // HALO
    vec = lambda n: _const_spec((n, d))
    tok = _tok_spec(tt, d)
    tok2 = pl.BlockSpec((2, 1, tt, d), lambda bi, ti: (0, bi, ti, 0))
    sd = jax.ShapeDtypeStruct((b, t, d), F32)
    sd2 = jax.ShapeDtypeStruct((2, b, t, d), F32)
    return pl.pallas_call(
        _rwkv_proj_kernel,
        grid=(b, t // tt),
        in_specs=[tok,
                  pl.BlockSpec((1, HALO, d), lambda bi, ti: (bi, jnp.maximum(ti * step - 1, 0), 0)),
                  pl.BlockSpec((1, HALO, d), lambda bi, ti: (bi, jnp.minimum((ti + 1) * step, nh - 1), 0)),
                  _mod_spec(d), vec(1),
                  _const_spec((d, 4 * d)), _const_spec((d, 2 * LORA)), _const_spec((2 * LORA, d)),
                  _const_spec((d, 2 * LORA)), _const_spec((2 * LORA, d)),
                  vec(3), vec(2), vec(2), vec(2), vec(1), vec(1), vec(2),
                  _const_spec(down.shape), _const_spec(up.shape)],
        out_specs=[tok, tok, tok, tok, tok, tok, tok2, tok2],
        out_shape=[sd, sd, sd, sd, sd, sd, sd2, sd2],
        scratch_shapes=[pltpu.VMEM((tt + 2 * HALO, d), F32), pltpu.VMEM((tt + 2 * HALO, d), F32)],
        compiler_params=pltpu.CompilerParams(dimension_semantics=("parallel", "parallel"),
                                             vmem_limit_bytes=VMEM_LIMIT),
        name="rwkv_proj",
    )(x, x, x, mod, g_pre, w_rkvz, w1c, w2c, a1c, a2c, mu_rkv, mu_wa, w0, a0, k_k, k_a, r_k, down, up)


def _short_conv(x, mod_ref, g_ref, w_ref, scw_ref, uext_ref, seg):
    tt, d = x.shape
    hb = _modulated(x, mod_ref, g_ref).astype(BF16)
    proj = lambda i: jnp.dot(hb, w_ref[:, i * d:(i + 1) * d], preferred_element_type=F32)
    u = proj(1) * proj(2)
    zero = jnp.zeros((HALO, d), F32)
    uext_ref[0:HALO] = zero
    uext_ref[HALO:HALO + tt] = u
    uext_ref[HALO + tt:] = zero
    pos = lax.broadcasted_iota(jnp.int32, (tt, d), 0) & (seg - 1)
    prev = jnp.where(pos > 0, uext_ref[pl.ds(HALO - 1, tt), :], 0.0)
    nxt = jnp.where(pos < seg - 1, uext_ref[pl.ds(HALO + 1, tt), :], 0.0)
    conv = scw_ref[0:1, :] * prev + scw_ref[1:2, :] * u + scw_ref[2:3, :] * nxt
    return proj(0) * conv * _silu(proj(3))


TICK_EVERY = 4


def _each(f, *lists, tick=None):
    out = []
    for n, xs in enumerate(zip(*lists)):
        out.append(f(*xs))
        if tick is not None and n % TICK_EVERY == TICK_EVERY - 1:
            tick()
    return out


def _stack_heads(z):
    first = (lax.broadcasted_iota(jnp.int32, z.shape, 1) & HEAD) == 0
    return jnp.concatenate([jnp.where(first, z, 0.0), jnp.where(first, 0.0, z)], axis=0)


def _pair_dot(x, y):
    return _dot(x, _stack_heads(y))


def _unit_triangular_solves(mats, rhs, tick):
    rows = lax.broadcasted_iota(jnp.int32, mats[0].shape, 0)
    cols = lax.broadcasted_iota(jnp.int32, mats[0].shape, 1) & (HEAD - 1)
    eye = (rows == cols).astype(F32)
    blk16 = (rows >> 4) == (cols >> 4)
    n = mats[0].shape[1]
    dg = [jnp.where(blk16, a, 0.0) for a in mats]
    off = _each(lambda a, d: a - d, mats, dg)
    wide = lambda x, y: _pair_dot(x, jnp.concatenate([x, y], axis=1))
    d2 = _each(_pair_dot, dg, dg, tick=tick)
    tm = [eye + d for d in dg]
    both = _each(wide, d2, tm, tick=tick)
    d4, tm = [x[:, 0:n] for x in both], _each(lambda t0, x: t0 + x[:, n:], tm, both)
    both = _each(wide, d4, tm, tick=tick)
    d8, tm = [x[:, 0:n] for x in both], _each(lambda t0, x: t0 + x[:, n:], tm, both)
    tm = _each(lambda p, t0: t0 + _pair_dot(p, t0), d8, tm, tick=tick)
    nb = _each(_pair_dot, tm, off, tick=tick)
    x = _each(_pair_dot, tm, rhs, tick=tick)
    nb2 = _each(_pair_dot, nb, nb, tick=tick)
    x = _each(lambda p, x0: x0 + _pair_dot(p, x0), nb, x, tick=tick)
    return _each(lambda p, x0: x0 + _pair_dot(p, x0), nb2, x, tick=tick)


def _wkv_kernel(s0_ref, r_ref, k_ref, v_ref, kk_ref, lw_ref, a_ref, ka_ref, o_ref, s_out, s_ref, *stage_refs,
                nt, reverse):
    t = pl.program_id(2)
    tt, width = r_ref.shape[1], r_ref.shape[2]
    nc, npair = tt // CHUNK, width // PAIR
    order = list(range(nc - 1, -1, -1) if reverse else range(nc))
    units = [(p, j) for j in order for p in range(npair)]
    half = len(stage_refs) // 2
    stage = (stage_refs[:half], stage_refs[half:])

    @pl.when(t == 0)
    def _():
        s_ref[...] = s0_ref[0]
        for ref in stage[1][:-1]:
            ref[...] = jnp.zeros(ref.shape, ref.dtype)
        stage[1][-1][...] = jnp.ones(stage[1][-1].shape, F32)

    def recurrence(refs):
        reader_ref, w0_ref, vs_ref, v_ref2, arbk_ref, bkg_ref, gend_ref = refs
        same_head = ((lax.broadcasted_iota(jnp.int32, (PAIR, PAIR), 0) >> 6)
                     == (lax.broadcasted_iota(jnp.int32, (PAIR, PAIR), 1) >> 6))
        s = [s_ref[p] for p in range(npair)]
        for i, (p, j) in enumerate(units):
            sh = _dot_nt(reader_ref[i], s[p])
            yield
            u = sh[0:CHUNK] + w0_ref[i]
            uv_stacked = jnp.concatenate([_stack_heads(u).astype(BF16), vs_ref[i]], axis=0)
            o_ref[0, j * CHUNK:(j + 1) * CHUNK, p * PAIR:(p + 1) * PAIR] = (
                sh[CHUNK:] + jnp.dot(arbk_ref[i], uv_stacked, preferred_element_type=F32))
            upd = _dot_tn(jnp.concatenate([u.astype(BF16), v_ref2[i]], axis=0), bkg_ref[i])
            s[p] = s[p] * gend_ref[i] + jnp.where(same_head, upd, 0.0)
            yield
        for p in range(npair):
            s_ref[p] = s[p]

    @pl.when(t == nt)
    def _():
        for _ in recurrence(stage[(nt - 1) % 2]):
            pass
        for p in range(npair):
            s_out[0, p] = s_ref[p]

    for parity in range(2):
        @pl.when((t < nt) & (t % 2 == parity))
        def _():
            steps = recurrence(stage[1 - parity])
            _wkv_prepare(r_ref, k_ref, v_ref, kk_ref, lw_ref, a_ref, ka_ref, stage[parity], units, reverse,
                         lambda: next(steps, None))
            for _ in steps:
                pass


def _wkv_prepare(r_ref, k_ref, v_ref, kk_ref, lw_ref, a_ref, ka_ref, stage, units, reverse, tick):
    reader_ref, w0_ref, vs_ref, v_ref2, arbk_ref, bkg_ref, gend_ref = stage
    tt = r_ref.shape[1]
    ri = lax.broadcasted_iota(jnp.int32, (CHUNK, PAIR), 0)
    ci = lax.broadcasted_iota(jnp.int32, (CHUNK, PAIR), 1) & (HEAD - 1)
    rt, ct = lax.broadcasted_iota(jnp.int32, (tt, tt), 0), lax.broadcasted_iota(jnp.int32, (tt, tt), 1)
    same_chunk = (rt >> 6) == (ct >> 6)
    if reverse:
        strict, incl, tri = ri < ci, ri <= ci, (same_chunk & (rt <= ct)).astype(BF16)
    else:
        strict, incl, tri = ri > ci, ri >= ci, (same_chunk & (rt >= ct)).astype(BF16)

    lw, a, kk = lw_ref[0, 0], a_ref[0, 0], kk_ref[0]
    cum = _dot_exact_x(tri, lw)
    g_inv = jnp.exp(-cum)
    a_t = -kk * jnp.exp(cum - lw)
    r_t = r_ref[0] * jnp.exp(cum)
    b_t = kk * a * g_inv
    k_t = k_ref[0] * (1.0 + (a - 1.0) * ka_ref[...]) * g_inv
    v_t = v_ref[0]

    cut = lambda z: [z[j * CHUNK:(j + 1) * CHUNK, p * PAIR:(p + 1) * PAIR] for p, j in units]
    a1, r1, b1, k1, v1 = cut(a_t), cut(r_t), cut(b_t), cut(k_t), cut(v_t)
    end_row = lambda j: j * CHUNK if reverse else (j + 1) * CHUNK - 1
    g_end = [jnp.exp(cum[end_row(j):end_row(j) + 1, p * PAIR:(p + 1) * PAIR]) for p, j in units]
    ar = _each(lambda x, y: jnp.concatenate([x, y], axis=0), a1, r1)
    bk = _each(lambda x, y: jnp.concatenate([x, y], axis=0), b1, k1)
    sc = _each(lambda x, b, k: _dot(x, jnp.concatenate([_stack_heads(b), _stack_heads(k)], axis=0).T),
               ar, b1, k1, tick=tick)
    a_ab = [jnp.where(strict, x[0:CHUNK, 0:PAIR], 0.0) for x in sc]
    a_ak = [jnp.where(strict, x[0:CHUNK, PAIR:], 0.0) for x in sc]
    a_rbk = [jnp.concatenate([jnp.where(incl, x[CHUNK:, 0:PAIR], 0.0), jnp.where(incl, x[CHUNK:, PAIR:], 0.0)],
                             axis=1).astype(BF16) for x in sc]
    w_ak = _each(_pair_dot, a_ak, v1, tick=tick)
    rhs = _each(lambda x, w: jnp.concatenate([x, w], axis=1), a1, w_ak)
    pre = _unit_triangular_solves(a_ab, rhs, tick)
    for i in range(len(units)):
        reader_ref[i] = jnp.concatenate([pre[i][:, 0:PAIR], r1[i]], axis=0).astype(BF16)
        w0_ref[i] = pre[i][:, PAIR:]
        vs_ref[i] = _stack_heads(v1[i]).astype(BF16)
        v_ref2[i] = v1[i].astype(BF16)
        arbk_ref[i] = a_rbk[i]
        bkg_ref[i] = (bk[i] * g_end[i]).astype(BF16)
        gend_ref[i] = g_end[i]


def _wkv(s0, r, k, v, kk, lw, a, k_a, reverse, tt):
    b, t, d = r.shape
    nt = t // tt
    z = 1 if reverse else 0
    npair = min(SCAN_PAIRS, d // PAIR)
    width = npair * PAIR
    n_units = npair * (tt // CHUNK)
    tpos = (lambda ti: nt - 1 - ti) if reverse else (lambda ti: ti)
    t_in = lambda ti: tpos(jnp.minimum(ti, nt - 1))
    t_out = lambda ti: tpos(jnp.maximum(ti - 1, 0))
    tok = pl.BlockSpec((1, tt, width), lambda bi, p, ti: (bi, t_in(ti), p))
    tokz = pl.BlockSpec((1, 1, tt, width), lambda bi, p, ti: (z, bi, t_in(ti), p))
    tok_out = pl.BlockSpec((1, tt, width), lambda bi, p, ti: (bi, t_out(ti), p))
    st = pl.BlockSpec((1, npair, PAIR, PAIR), lambda bi, p, ti: (bi, p, 0, 0))
    return pl.pallas_call(
        functools.partial(_wkv_kernel, nt=nt, reverse=reverse),
        grid=(b, d // width, nt + 1),
        in_specs=[st, tok, tok, tok, tok, tokz, tokz, pl.BlockSpec((1, width), lambda bi, p, ti: (0, p))],
        out_specs=[tok_out, st],
        out_shape=[jax.ShapeDtypeStruct((b, t, d), F32), jax.ShapeDtypeStruct((b, d // PAIR, PAIR, PAIR), F32)],
        scratch_shapes=[pltpu.VMEM((npair, PAIR, PAIR), F32)] + 2 * [
            pltpu.VMEM((n_units, 2 * CHUNK, PAIR), BF16), pltpu.VMEM((n_units, CHUNK, PAIR), F32),
            pltpu.VMEM((n_units, 2 * CHUNK, PAIR), BF16), pltpu.VMEM((n_units, CHUNK, PAIR), BF16),
            pltpu.VMEM((n_units, CHUNK, 2 * PAIR), BF16), pltpu.VMEM((n_units, 2 * CHUNK, PAIR), BF16),
            pltpu.VMEM((n_units, 1, PAIR), F32)],
        compiler_params=pltpu.CompilerParams(dimension_semantics=("parallel", "parallel", "arbitrary"),
                                             vmem_limit_bytes=VMEM_LIMIT),
        name="wkv_bwd" if reverse else "wkv_fwd",
    )(s0, r, k, v, kk, lw, a, k_a)


def _even_out_kernel(x_ref, of_ref, ob_ref, bonus_ref, sza_ref, mod_ref, g_ref, wsc_ref, scw_ref, lnw_ref, lnb_ref,
                     gp_ref, w_ref, down_ref, up_ref, o_ref, uext_ref, *, seg):
    d = x_ref.shape[2]
    yb = _short_conv(x_ref[0], mod_ref, g_ref, wsc_ref, scw_ref, uext_ref, seg)
    down, up = down_ref[...], up_ref[...]
    o = of_ref[0] + ob_ref[0]
    dev = o - _head_sums(o, down, up) * (1.0 / HEAD)
    var = _head_sums(dev * dev, down, up) * (1.0 / HEAD)
    ya = (dev * lax.rsqrt(var + GN_EPS) * lnw_ref[...] + lnb_ref[...] + bonus_ref[0]) * sza_ref[0]
    y = _dot(ya, w_ref[0:d, :]) + _dot(yb, w_ref[d:2 * d, :])
    o_ref[0] = x_ref[0] + mod_ref[0, 2:3, :] * _rms(y, gp_ref[...])


def _even_out(x, o_f, o_b, bonus, sza, mod, g_pre, w_sc, sc_w, lnx_w, lnx_b, g_post, w_out, down, up, seg, tt):
    b, t, d = x.shape
    tok = _tok_spec(tt, d)
    vec = _const_spec((1, d))
    return pl.pallas_call(
        functools.partial(_even_out_kernel, seg=seg),
        grid=(b, t // tt),
        in_specs=[tok, tok, tok, tok, tok, _mod_spec(d), vec, _const_spec((d, 4 * d)), _const_spec((3, d)),
                  vec, vec, vec, _const_spec((2 * d, d)), _const_spec(down.shape), _const_spec(up.shape)],
        out_specs=tok,
        out_shape=jax.ShapeDtypeStruct((b, t, d), F32),
        scratch_shapes=[pltpu.VMEM((tt + 2 * HALO, d), F32)],
        compiler_params=pltpu.CompilerParams(dimension_semantics=("parallel", "parallel"),
                                             vmem_limit_bytes=VMEM_LIMIT),
        name="even_out",
    )(x, o_f, o_b, bonus, sza, mod, g_pre, w_sc, sc_w, lnx_w, lnx_b, g_post, w_out, down, up)


def _conformer_kernel(x_ref, mod_ref, g_ref, w_in_ref, dww_ref, dwb_ref, lnw_ref, lnb_ref, w_out_ref, gp_ref,
                      o_ref, pad_ref, conv_ref, shift_ref, *, seg):
    tt, d = x_ref.shape[1], x_ref.shape[2]
    e = dww_ref.shape[1]
    taps = dww_ref.shape[0]
    span = seg + 2 * CONV_PAD
    x = x_ref[0]
    hb = _modulated(x, mod_ref, g_ref).astype(BF16)
    proj = lambda i: jnp.dot(hb, w_in_ref[:, i * e:(i + 1) * e], preferred_element_type=F32)
    glu = proj(0) * _sigmoid(proj(1))
    zero = jnp.zeros((CONV_PAD, e), F32)
    for s in range(tt // seg):
        pad_ref[s * span:s * span + CONV_PAD] = zero
        pad_ref[s * span + CONV_PAD:s * span + CONV_PAD + seg] = glu[s * seg:(s + 1) * seg]
        pad_ref[s * span + CONV_PAD + seg:(s + 1) * span] = zero

    rows = min(seg, CHUNK)
    lead = CONV_PAD - taps // 2
    win = shift_ref.shape[1]
    for s in range(tt // seg):
        for r0 in range(0, seg, rows):
            for c0 in range(0, e, CONV_COLS):
                cols = pl.ds(c0, CONV_COLS)
                for phase in range(8):
                    shift_ref[phase] = pad_ref[pl.ds(s * span + r0 + phase, win), cols]
                acc = jnp.zeros((rows, CONV_COLS), F32)
                for j in range(taps):
                    q, phase = divmod(lead + j, 8)
                    acc = acc + dww_ref[j:j + 1, cols] * shift_ref[phase, pl.ds(8 * q, rows), :]
                conv_ref[pl.ds(s * seg + r0, rows), cols] = acc

    conv = conv_ref[...] + dwb_ref[...]
    dev = conv - jnp.mean(conv, axis=-1, keepdims=True)
    var = jnp.mean(dev * dev, axis=-1, keepdims=True)
    act = _silu(dev * lax.rsqrt(var + LN_EPS) * lnw_ref[...] + lnb_ref[...]) * _silu(proj(2))
    o_ref[0] = x + mod_ref[0, 2:3, :] * _rms(_dot(act, w_out_ref[...]), gp_ref[...])


def _conformer(x, mod, g_pre, w_in, dw_w, dw_b, ln_w, ln_b, w_out, g_post, seg, tt):
    b, t, d = x.shape
    taps, e = dw_w.shape
    assert taps // 2 <= CONV_PAD and tt % seg == 0
    rows = min(seg, CHUNK)
    win = rows + 8 * ((CONV_PAD + taps // 2) // 8)
    assert 7 + win <= rows + 2 * CONV_PAD
    tok = _tok_spec(tt, d)
    return pl.pallas_call(
        functools.partial(_conformer_kernel, seg=seg),
        grid=(b, t // tt),
        in_specs=[tok, _mod_spec(d), _const_spec((1, d)), _const_spec((d, 3 * e)), _const_spec((taps, e)),
                  _const_spec((1, e)), _const_spec((1, e)), _const_spec((1, e)), _const_spec((e, d)),
                  _const_spec((1, d))],
        out_specs=tok,
        out_shape=jax.ShapeDtypeStruct((b, t, d), F32),
        scratch_shapes=[pltpu.VMEM(((tt // seg) * (seg + 2 * CONV_PAD), e), F32), pltpu.VMEM((tt, e), F32),
                        pltpu.VMEM((8, win, CONV_COLS), F32)],
        compiler_params=pltpu.CompilerParams(dimension_semantics=("parallel", "parallel"),
                                             vmem_limit_bytes=VMEM_LIMIT),
        name="conformer",
    )(x, mod, g_pre, w_in, dw_w, dw_b, ln_w, ln_b, w_out, g_post)


def _conformer_tm_kernel(x_ref, mod_ref, g_ref, w_in_ref, dww_ref, dwb_ref, lnw_ref, lnb_ref, w_out_ref, gp_ref,
                         o_ref, pad_ref, conv_ref, gate_ref):
    seg, nseg, d = x_ref.shape
    n = seg * nseg
    taps, e = dww_ref.shape[0], dww_ref.shape[2]
    x = x_ref[...].reshape(n, d)
    hb = _modulated(x, mod_ref, g_ref).astype(BF16)
    proj = lambda lo, hi: jnp.dot(hb, w_in_ref[:, lo:hi], preferred_element_type=F32)
    zero = jnp.zeros((CONV_PAD, nseg, e), F32)
    pad_ref[0:CONV_PAD] = zero
    pad_ref[CONV_PAD + seg:] = zero
    for c0 in range(0, e, CONV_COLS):
        glu = proj(c0, c0 + CONV_COLS) * _sigmoid(proj(e + c0, e + c0 + CONV_COLS))
        pad_ref[CONV_PAD:CONV_PAD + seg, :, c0:c0 + CONV_COLS] = glu.reshape(seg, nseg, CONV_COLS)

    lead = CONV_PAD - taps // 2

    for c0 in range(0, e, CONV_COLS):
        cols = pl.ds(c0, CONV_COLS)
        for t0 in range(0, seg, 8):
            acc = jnp.zeros((8, nseg, CONV_COLS), F32)
            for j in range(taps):
                acc = acc + dww_ref[j, :, cols][None] * pad_ref[pl.ds(t0 + lead + j, 8), :, cols]
            conv_ref[pl.ds(t0, 8), :, cols] = acc
        gate_ref[:, c0:c0 + CONV_COLS] = _silu(proj(2 * e + c0, 2 * e + c0 + CONV_COLS))

    conv = conv_ref[...].reshape(n, e) + dwb_ref[...]
    dev = conv - jnp.mean(conv, axis=-1, keepdims=True)
    var = jnp.mean(dev * dev, axis=-1, keepdims=True)
    act = _silu(dev * lax.rsqrt(var + LN_EPS) * lnw_ref[...] + lnb_ref[...]) * gate_ref[...]
    out = x + mod_ref[0, 2:3, :] * _rms(_dot(act, w_out_ref[...]), gp_ref[...])
    o_ref[...] = out.reshape(seg, nseg, d)


def _conformer_tm(x, mod, g_pre, w_in, dw_w, dw_b, ln_w, ln_b, w_out, g_post, seg):
    b, t, d = x.shape
    taps, e = dw_w.shape
    nseg = 8
    groups = t // (seg * nseg)
    assert taps // 2 <= CONV_PAD and seg % 8 == 0 and groups * seg * nseg == t
    x5 = x.reshape(b, seg, groups, nseg, d)
    tile = pl.BlockSpec((None, seg, None, nseg, d), lambda bi, gi: (bi, 0, gi, 0, 0))
    out = pl.pallas_call(
        _conformer_tm_kernel,
        grid=(b, groups),
        in_specs=[tile, _mod_spec(d), _const_spec((1, d)), _const_spec((d, 3 * e)), _const_spec((taps, nseg, e)),
                  _const_spec((1, e)), _const_spec((1, e)), _const_spec((1, e)), _const_spec((e, d)),
                  _const_spec((1, d))],
        out_specs=tile,
        out_shape=jax.ShapeDtypeStruct(x5.shape, F32),
        scratch_shapes=[pltpu.VMEM((seg + 2 * CONV_PAD, nseg, e), F32), pltpu.VMEM((seg, nseg, e), F32),
                        pltpu.VMEM((seg * nseg, e), F32)],
        compiler_params=pltpu.CompilerParams(dimension_semantics=("parallel", "parallel"),
                                             vmem_limit_bytes=VMEM_LIMIT),
        name="conformer_tm",
    )(x5, mod, g_pre, w_in, jnp.broadcast_to(dw_w[:, None, :], (taps, nseg, e)), dw_b, ln_w, ln_b, w_out, g_post)
    return out.reshape(b, t, d)


def _to_col_major(h, rows):
    b, t, ch = h.shape
    return h.reshape(b, rows, GRID_W, ch).swapaxes(1, 2).reshape(b, t, ch)


def _to_row_major(h, rows):
    b, t, ch = h.shape
    return h.reshape(b, GRID_W, rows, ch).swapaxes(1, 2).reshape(b, t, ch)


def kernel(x, c, ctx, c_ctx, mod_w, mod_b, g_pre, g_post, ev_w_in, ev_w_out, ev_mu_rkv, ev_mu_wa, ev_w0, ev_w1, ev_w2, ev_a0, ev_a1, ev_a2, ev_k_k, ev_k_a, ev_r_k, ev_lnx_w, ev_lnx_b, ev_sc_w, od_w_in, od_dw_w, od_dw_b, od_ln_w, od_ln_b, od_w_out):
    b, t, d = x.shape
    n_ctx = ctx.shape[1]
    depth = mod_w.shape[0]
    rows = t // GRID_W
    assert d % PAIR == 0 and t % TOKEN_TILE == 0 and n_ctx % CHUNK == 0
    tt_ctx = min(TOKEN_TILE, n_ctx)

    mod_rows = -(-(b + 1) // 8) * 8
    c_all = jnp.zeros((mod_rows, d), F32).at[:b].set(c).at[b].set(c_ctx)
    mods = _modulation(c_all, mod_w, mod_b).reshape(depth, mod_rows, 3, d)
    head_cols = -(-(d // HEAD) // 128) * 128
    down = (jnp.arange(d)[:, None] // HEAD == jnp.arange(head_cols)[None, :]).astype(BF16)
    sums = (down, down.T)

    col_major = False
    for i in range(depth):
        j = i // 2
        last = i == depth - 1
        col = j % 2 == 1
        want_col_major = col if i % 2 == 0 else not col
        if want_col_major != col_major:
            x = _to_col_major(x, rows) if want_col_major else _to_row_major(x, rows)
            col_major = want_col_major
        seg = rows if col else GRID_W
        mod_x = mods[i, :b]
        mod_c = jnp.broadcast_to(mods[i, b], (b, 3, d))
        gpre, gpost = g_pre[i][None], g_post[i][None]
        if i % 2 == 0:
            w_in = ev_w_in[j].astype(BF16)
            w_rkvz, w_sc = w_in[:, :4 * d], w_in[:, 4 * d:]
            w_out = ev_w_out[j].astype(BF16)
            cat = lambda w: jnp.concatenate([w[0], w[1]], axis=1).astype(BF16)
            stk = lambda w: jnp.concatenate([w[0], w[1]], axis=0).astype(BF16)
            k_a = ev_k_a[j][None]
            proj_args = (gpre, w_rkvz, cat(ev_w1[j]), stk(ev_w2[j]), cat(ev_a1[j]), stk(ev_a2[j]), ev_mu_rkv[j],
                         ev_mu_wa[j], ev_w0[j], ev_a0[j], ev_k_k[j][None], k_a, ev_r_k[j].reshape(2, d), *sums)
            pc = _rwkv_proj(ctx, mod_c, *proj_args, tt_ctx)
            px = _rwkv_proj(x, mod_x, *proj_args, TOKEN_TILE)
            zero = jnp.zeros((b, d // PAIR, PAIR, PAIR), F32)
            scan = lambda p, s0, rev, tt: _wkv(s0, p[0], p[1], p[2], p[3], p[6], p[7], k_a, rev, tt)
            oc_f, s_f = scan(pc, zero, False, min(SCAN_TILE, n_ctx))
            oc_b, s_b = scan(pc, zero, True, min(SCAN_TILE, n_ctx))
            ox_f, _ = scan(px, s_f, False, SCAN_TILE)
            ox_b, _ = scan(px, s_b, True, SCAN_TILE)
            out_args = (gpre, w_sc, ev_sc_w[j], ev_lnx_w[j][None], ev_lnx_b[j][None], gpost, w_out, *sums)
            x = _even_out(x, ox_f, ox_b, px[5], px[4], mod_x, *out_args, seg, TOKEN_TILE)
            if not last:
                ctx = _even_out(ctx, oc_f, oc_b, pc[5], pc[4], mod_c, *out_args, n_ctx, tt_ctx)
        else:
            odd_args = (gpre, od_w_in[j].astype(BF16), od_dw_w[j], od_dw_b[j][None], od_ln_w[j][None],
                        od_ln_b[j][None], od_w_out[j].astype(BF16), gpost)
            if not last:
                ctx_new = _conformer(ctx, mod_c, *odd_args, n_ctx, tt_ctx)
            x = _conformer_tm(x, mod_x, *odd_args, seg)
            if not last:
                ctx = ctx_new
    if col_major:
        x = _to_row_major(x, rows)
    return x
```
